```python
import jax, jax.numpy as jnp
from jax import lax
import numpy as np

D_MODEL = 4096
BATCH = 4
SEQ = 2048
DEPTH = 1
DEC_BATCH = 128
DEC_SEQ = 4
PAST_LEN = 16384
PAGE_SIZE = 128

HEAD_DIM = 128
NSA_HEADS = 16
CMP_BLOCK = 32
SEL_BLOCK = 64
N_SELECT = 16
WINDOW = 512
MLA_HEADS = 16
MLA_Q_LORA = 896
MLA_KV_LORA = 320
MLA_NOPE = 128
MLA_ROPE = 64
MLA_V = 128
MLA_ROW = MLA_KV_LORA + MLA_ROPE
N_GROUPS = 8
EXPERTS_PER_GROUP = 8
N_EXPERTS = N_GROUPS * EXPERTS_PER_GROUP
TOPK_IN_GROUP = 2
D_EXPERT = 1024
PLE_DIM = 256
ROPE_THETA = 10000.0
RMS_EPS = 1e-6
QUERY_BLOCK = 128
EXPERT_BLOCK = 128
FORCE_BONUS = 1e4
NEG_INF = -1e30
TINY = 1e-30
NSA_SCALE = HEAD_DIM ** -0.5
MLA_SCALE = (MLA_NOPE + MLA_ROPE) ** -0.5
NSA_Q_COLS = NSA_HEADS * HEAD_DIM
NSA_KV_COLS = 3 * 2 * HEAD_DIM
NSA_GATE_COLS = 3 * NSA_HEADS
IN_COLS = NSA_Q_COLS + NSA_KV_COLS + NSA_GATE_COLS + MLA_Q_LORA + MLA_ROW
GROUP_WIDTH = NSA_HEADS * HEAD_DIM

kernel_name = 'hymba_nsa_mla_hmoe_ple_step'

F32 = jnp.float32


def rms_norm(x, g):
    xf = x.astype(F32)
    y = xf * lax.rsqrt(jnp.mean(xf * xf, axis=-1, keepdims=True) + RMS_EPS)
    return (y * g.astype(F32)).astype(x.dtype)


def apply_rope(x, pos):
    dim = x.shape[-1]
    inv = ROPE_THETA ** (-jnp.arange(0, dim, 2, dtype=F32) / dim)
    ang = pos.astype(F32)[:, None] * inv[None, :]
    cos, sin = jnp.cos(ang), jnp.sin(ang)
    if x.ndim == 4:
        cos, sin = cos[:, None, :], sin[:, None, :]
    x1, x2 = jnp.split(x.astype(F32), 2, axis=-1)
    return jnp.concatenate([x1 * cos - x2 * sin, x2 * cos + x1 * sin], -1).astype(x.dtype)


def masked_softmax(s, mask):
    s = jnp.where(mask, s, NEG_INF)
    m = jnp.max(s, axis=-1, keepdims=True)
    e = jnp.where(mask, jnp.exp(s - m), 0.0)
    return e / jnp.maximum(jnp.sum(e, axis=-1, keepdims=True), TINY)


def mixer_inputs(a, pos, w_in, mla_q_norm, mla_kv_norm, mla_w_uq, mla_w_uk):
    B, S, _ = a.shape
    z = a @ w_in
    c0 = NSA_Q_COLS
    c1 = c0 + NSA_KV_COLS
    c2 = c1 + NSA_GATE_COLS
    c3 = c2 + MLA_Q_LORA
    q = apply_rope(z[..., :c0].reshape(B, S, NSA_HEADS, HEAD_DIM), pos)
    kv = z[..., c0:c1].reshape(B, S, 3, 2, HEAD_DIM)
    k = apply_rope(kv[..., 0, :], pos)
    kv = jnp.stack([k, kv[..., 1, :]], axis=3)
    gates = jax.nn.sigmoid(z[..., c1:c2].astype(F32)).reshape(B, S, 3, NSA_HEADS)
    c_q = rms_norm(z[..., c2:c3], mla_q_norm)
    q_mla = jnp.einsum('bsc,chd->bshd', c_q, mla_w_uq)
    q_lat = jnp.einsum('bshd,chd->bshc', q_mla[..., :MLA_NOPE], mla_w_uk)
    q_pe = apply_rope(q_mla[..., MLA_NOPE:], pos)
    c_kv = rms_norm(z[..., c3:c3 + MLA_KV_LORA], mla_kv_norm)
    k_pe = apply_rope(z[..., c3 + MLA_KV_LORA:], pos)
    mla_row = jnp.concatenate([c_kv, k_pe], -1)
    return q, kv[:, :, 0], kv[:, :, 1], kv[:, :, 2], gates, q_lat, q_pe, mla_row


def compress(rows, cmp_pos, cmp_w):
    B, L = rows.shape[:2]
    nb = L // CMP_BLOCK
    blk = rows[:, :nb * CMP_BLOCK].reshape(B, nb, CMP_BLOCK, 2, HEAD_DIM) + jnp.swapaxes(cmp_pos, 0, 1)
    return jnp.einsum('bnlcd,cldk->bnck', blk, cmp_w)


def cmp_attend(q, qpos, ck):
    s = jnp.einsum('bshd,bnd->bshn', q, ck[:, :, 0], preferred_element_type=F32) * NSA_SCALE
    nc = ck.shape[1]
    mask = ((jnp.arange(nc) + 1) * CMP_BLOCK - 1)[None, :] <= qpos[:, None]
    p = masked_softmax(s, mask[None, :, None, :])
    return jnp.einsum('bshn,bnd->bshd', p, ck[:, :, 1]), p


def select_blocks(p_cmp, qpos, n_sel):
    B, S, _, nc = p_cmp.shape
    ratio = SEL_BLOCK // CMP_BLOCK
    imp = jnp.sum(p_cmp, axis=2)
    imp = jnp.pad(imp, ((0, 0), (0, 0), (0, n_sel * ratio - nc))).reshape(B, S, n_sel, ratio).sum(-1)
    blk = jnp.arange(n_sel)[None, :]
    cur = (qpos // SEL_BLOCK)[:, None]
    valid = blk * SEL_BLOCK <= qpos[:, None]
    forced = (blk == 0) | (blk == cur) | (blk == cur - 1)
    score = jnp.where(valid, imp + jnp.where(forced, FORCE_BONUS, 0.0), -jnp.inf)
    top, idx = lax.top_k(score, min(N_SELECT, n_sel))
    return idx, jnp.isfinite(top)


def sel_attend(q, qpos, g, idx, sv):
    B, S, k, sb = g.shape[:4]
    s = jnp.einsum('bshd,bsknd->bshkn', q, g[..., 0, :], preferred_element_type=F32) * NSA_SCALE
    kpos = idx[..., None] * SEL_BLOCK + jnp.arange(SEL_BLOCK)
    mask = sv[..., None] & (kpos <= qpos[None, :, None, None])
    p = masked_softmax(s.reshape(B, S, NSA_HEADS, k * sb), mask.reshape(B, S, 1, k * sb))
    return jnp.einsum('bshm,bsmd->bshd', p, g[..., 1, :].reshape(B, S, k * sb, HEAD_DIM))


def win_attend(q, qpos, kw, kpos):
    s = jnp.einsum('bshd,bkd->bshk', q, kw[:, :, 0], preferred_element_type=F32) * NSA_SCALE
    kp = kpos[None, :]
    mask = (kp <= qpos[:, None]) & (kp > qpos[:, None] - WINDOW) & (kp >= 0)
    p = masked_softmax(s, mask[None, :, None, :])
    return jnp.einsum('bshk,bkd->bshd', p, kw[:, :, 1])


def gate_branches(gates, o_c, o_s, o_w):
    g = gates[..., None]
    return g[:, :, 0] * o_c + g[:, :, 1] * o_s + g[:, :, 2] * o_w


def nsa_prompt(q, kv_c, kv_s, kv_w, gates, cmp_pos, cmp_w):
    B, T = q.shape[:2]
    ck = compress(kv_c, cmp_pos, cmp_w)
    n_sel = T // SEL_BLOCK
    sel_blocks = kv_s.reshape(B, n_sel, SEL_BLOCK, 2, HEAD_DIM)
    win_pad = jnp.pad(kv_w, ((0, 0), (WINDOW, 0), (0, 0), (0, 0)))
    b_ix = jnp.arange(B)[:, None, None]

    def one_block(i):
        start = i * QUERY_BLOCK
        qb = lax.dynamic_slice_in_dim(q, start, QUERY_BLOCK, axis=1)
        gb = lax.dynamic_slice_in_dim(gates, start, QUERY_BLOCK, axis=1)
        qpos = start + jnp.arange(QUERY_BLOCK)
        o_c, p_c = cmp_attend(qb, qpos, ck)
        idx, sv = select_blocks(p_c, qpos, n_sel)
        o_s = sel_attend(qb, qpos, sel_blocks[b_ix, idx], idx, sv)
        kw = lax.dynamic_slice_in_dim(win_pad, start, WINDOW + QUERY_BLOCK, axis=1)
        kpos = start - WINDOW + jnp.arange(WINDOW + QUERY_BLOCK)
        o_w = win_attend(qb, qpos, kw, kpos)
        return gate_branches(gb, o_c, o_s, o_w)

    o = lax.map(one_block, jnp.arange(T // QUERY_BLOCK))
    return jnp.swapaxes(o, 0, 1).reshape(B, T, NSA_HEADS, HEAD_DIM)


def nsa_sample(q, kv_c, kv_s, kv_w, gates, cache_cmp, cache_sel, win_buf, page_table, layer, cmp_pos, cmp_w):
    DB, S = q.shape[:2]
    n_pages = page_table.shape[1]
    P = n_pages * PAGE_SIZE
    qpos = P + jnp.arange(S)
    past_c = cache_cmp[layer, page_table].reshape(DB, P, 2, HEAD_DIM)
    ck = jnp.concatenate([compress(past_c, cmp_pos, cmp_w), compress(kv_c, cmp_pos, cmp_w)], axis=1)
    o_c, p_c = cmp_attend(q, qpos, ck)
    n_sel = -(-(P + S) // SEL_BLOCK)
    idx, sv = select_blocks(p_c, qpos, n_sel)
    bpp = PAGE_SIZE // SEL_BLOCK
    n_past_blk = P // SEL_BLOCK
    b_ix = jnp.arange(DB)[:, None, None]
    phys = page_table[b_ix, jnp.minimum(idx // bpp, n_pages - 1)]
    offs = (idx % bpp)[..., None] * SEL_BLOCK + jnp.arange(SEL_BLOCK)
    g_past = cache_sel[layer, phys[..., None], offs]
    n_new_blk = -(-S // SEL_BLOCK)
    new_pad = jnp.pad(kv_s, ((0, 0), (0, n_new_blk * SEL_BLOCK - S), (0, 0), (0, 0)))
    new_pad = new_pad.reshape(DB, n_new_blk, SEL_BLOCK, 2, HEAD_DIM)
    g_new = new_pad[b_ix, jnp.clip(idx - n_past_blk, 0, n_new_blk - 1)]
    g = jnp.where((idx < n_past_blk)[..., None, None, None], g_past, g_new)
    o_s = sel_attend(q, qpos, g, idx, sv)
    wb = win_buf.shape[1]
    kw = jnp.concatenate([win_buf, kv_w], axis=1)
    kpos = jnp.concatenate([P - wb + jnp.arange(wb), qpos])
    o_w = win_attend(q, qpos, kw, kpos)
    new_buf = kw[:, kw.shape[1] - min(WINDOW, wb + S):]
    return gate_branches(gates, o_c, o_s, o_w), new_buf


def mla_prompt(q_lat, q_pe, rows):
    B, T = q_lat.shape[:2]
    c, r = rows[..., :MLA_KV_LORA], rows[..., MLA_KV_LORA:]
    kpos = jnp.arange(T)

    def one_block(i):
        start = i * QUERY_BLOCK
        ql = lax.dynamic_slice_in_dim(q_lat, start, QUERY_BLOCK, axis=1)
        qr = lax.dynamic_slice_in_dim(q_pe, start, QUERY_BLOCK, axis=1)
        qpos = start + jnp.arange(QUERY_BLOCK)
        s = (jnp.einsum('bqhc,bkc->bhqk', ql, c, preferred_element_type=F32)
             + jnp.einsum('bqhr,bkr->bhqk', qr, r, preferred_element_type=F32)) * MLA_SCALE
        p = masked_softmax(s, (kpos[None, :] <= qpos[:, None])[None, None])
        return jnp.einsum('bhqk,bkc->bqhc', p, c)

    o = lax.map(one_block, jnp.arange(T // QUERY_BLOCK))
    return jnp.swapaxes(o, 0, 1).reshape(B, T, MLA_HEADS, MLA_KV_LORA)


def mla_sample(q_lat, q_pe, new_rows, cache_mla, page_table, layer):
    DB, S = q_lat.shape[:2]
    P = page_table.shape[1] * PAGE_SIZE
    past = cache_mla[layer, page_table].reshape(DB, P, MLA_ROW)

    def scores(rows):
        return (jnp.einsum('bqhc,bkc->bhqk', q_lat, rows[..., :MLA_KV_LORA], preferred_element_type=F32)
                + jnp.einsum('bqhr,bkr->bhqk', q_pe, rows[..., MLA_KV_LORA:], preferred_element_type=F32)) * MLA_SCALE

    s = jnp.concatenate([scores(past), scores(new_rows)], -1)
    causal_new = jnp.arange(S)[None, :] <= jnp.arange(S)[:, None]
    mask = jnp.concatenate([jnp.ones((S, P), bool), causal_new], -1)
    p = masked_softmax(s, mask[None, None])
    return (jnp.einsum('bhqk,bkc->bqhc', p[..., :P], past[..., :MLA_KV_LORA])
            + jnp.einsum('bhqk,bkc->bqhc', p[..., P:], new_rows[..., :MLA_KV_LORA]))


def mixer_output(o_nsa, o_lat, mla_w_uv, group_norm, w_out, dtype):
    B, S = o_nsa.shape[:2]
    o_mla = jnp.einsum('bshc,chd->bshd', o_lat, mla_w_uv)
    a = rms_norm(o_nsa.reshape(B, S, GROUP_WIDTH), group_norm[0])
    b = rms_norm(o_mla.reshape(B, S, GROUP_WIDTH), group_norm[1])
    return (jnp.concatenate([a, b], -1).astype(dtype) @ w_out).astype(dtype)


def grouped_expert_ffn(x2, expert, gates, w_gate, w_up, w_down):
    n, d = x2.shape
    k = expert.shape[1]
    nk = n * k
    flat_e = expert.reshape(-1)
    order = jnp.argsort(flat_e)
    sorted_e = flat_e[order]
    counts = jnp.bincount(flat_e, length=N_EXPERTS)
    starts = jnp.cumsum(counts) - counts
    padded = (counts + EXPERT_BLOCK - 1) // EXPERT_BLOCK * EXPERT_BLOCK
    pad_ends = jnp.cumsum(padded)
    pad_starts = pad_ends - padded
    dest = pad_starts[sorted_e] + jnp.arange(nk) - starts[sorted_e]
    n_blocks = -(-(nk + N_EXPERTS * (EXPERT_BLOCK - 1)) // EXPERT_BLOCK)
    rows = n_blocks * EXPERT_BLOCK
    src = jnp.full((rows,), n, jnp.int32).at[dest].set((order // k).astype(jnp.int32))
    x_pad = jnp.concatenate([x2, jnp.zeros((1, d), x2.dtype)], 0)
    xb = x_pad[src].reshape(n_blocks, EXPERT_BLOCK, d)
    block_e = jnp.minimum(jnp.searchsorted(pad_ends, jnp.arange(n_blocks) * EXPERT_BLOCK, side='right'), N_EXPERTS - 1)

    def run(args):
        xr, e = args
        return (jax.nn.silu(xr @ w_gate[e]) * (xr @ w_up[e])) @ w_down[e]

    yb = lax.map(run, (xb, block_e)).reshape(rows, d)
    y_sorted = yb[dest] * gates.reshape(-1)[order][:, None]
    return jnp.zeros((n, d), y_sorted.dtype).at[order // k].add(y_sorted)


def hier_moe(a, rg_w, rg_b, re_w, re_b, w_gate, w_up, w_down):
    B, S, D = a.shape
    x2 = a.reshape(-1, D)
    n = x2.shape[0]
    g_logits = (x2 @ rg_w + rg_b).astype(F32)
    g_idx = jnp.argmax(g_logits, -1)
    g_gate = jnp.take_along_axis(jax.nn.softmax(g_logits, -1), g_idx[:, None], 1)[:, 0]
    e_logits = (x2 @ re_w + re_b).astype(F32).reshape(n, N_GROUPS, EXPERTS_PER_GROUP)
    e_logits = jnp.take_along_axis(e_logits, g_idx[:, None, None], 1)[:, 0]
    top_v, top_i = lax.top_k(e_logits, TOPK_IN_GROUP)
    gates = g_gate[:, None] * jax.nn.softmax(top_v, -1)
    expert = g_idx[:, None] * EXPERTS_PER_GROUP + top_i
    y = grouped_expert_ffn(x2, expert, gates, w_gate, w_up, w_down)
    return y.reshape(B, S, D).astype(a.dtype)


def ple_add(h, p_i, ple_proj, ple_norm, ple_gate_w):
    gate = jax.nn.sigmoid((rms_norm(h, ple_norm) @ ple_gate_w).astype(F32))
    return (h + gate * (p_i @ ple_proj)).astype(h.dtype)


def setup_inputs(seed: int = 0) -> dict:
    key = jax.random.key(seed)
    ks = jax.random.split(key, 40)
    n_pages = PAST_LEN // PAGE_SIZE
    n_pool = (DEC_BATCH * n_pages * 5) // 4
    win_buf = min(WINDOW, PAST_LEN)

    def nrm(k, shape, scale):
        return jax.random.normal(k, shape, F32) * scale

    def gain(k, shape):
        return 1.0 + 0.02 * jax.random.normal(k, shape, F32)

    page_table = jax.random.permutation(ks[8], n_pool)[:DEC_BATCH * n_pages].reshape(DEC_BATCH, n_pages).astype(jnp.int32)
    return {
        'x_prompt': nrm(ks[0], (BATCH, SEQ, D_MODEL), 1.0),
        'x_sample': nrm(ks[1], (DEC_BATCH, DEC_SEQ, D_MODEL), 1.0),
        'p_prompt': nrm(ks[2], (DEPTH, BATCH, SEQ, PLE_DIM), 1.0),
        'p_sample': nrm(ks[3], (DEPTH, DEC_BATCH, DEC_SEQ, PLE_DIM), 1.0),
        'cache_nsa_cmp': nrm(ks[4], (DEPTH, n_pool, PAGE_SIZE, 2, HEAD_DIM), 1.0),
        'cache_nsa_sel': nrm(ks[5], (DEPTH, n_pool, PAGE_SIZE, 2, HEAD_DIM), 1.0),
        'cache_mla': nrm(ks[6], (DEPTH, n_pool, PAGE_SIZE, MLA_ROW), 1.0),
        'state_nsa_win': nrm(ks[7], (DEPTH, DEC_BATCH, win_buf, 2, HEAD_DIM), 1.0),
        'page_table': page_table,
        'norm_attn': gain(ks[9], (DEPTH, D_MODEL)),
        'w_in': nrm(ks[10], (DEPTH, D_MODEL, IN_COLS), D_MODEL ** -0.5),
        'nsa_cmp_pos': nrm(ks[11], (DEPTH, 2, CMP_BLOCK, HEAD_DIM), 0.1),
        'nsa_cmp_w': nrm(ks[12], (DEPTH, 2, CMP_BLOCK, HEAD_DIM, HEAD_DIM), (CMP_BLOCK * HEAD_DIM) ** -0.5),
        'mla_q_norm': gain(ks[13], (DEPTH, MLA_Q_LORA)),
        'mla_kv_norm': gain(ks[14], (DEPTH, MLA_KV_LORA)),
        'mla_w_uq': nrm(ks[15], (DEPTH, MLA_Q_LORA, MLA_HEADS, MLA_NOPE + MLA_ROPE), MLA_Q_LORA ** -0.5),
        'mla_w_uk': nrm(ks[16], (DEPTH, MLA_KV_LORA, MLA_HEADS, MLA_NOPE), MLA_KV_LORA ** -0.5),
        'mla_w_uv': nrm(ks[17], (DEPTH, MLA_KV_LORA, MLA_HEADS, MLA_V), MLA_KV_LORA ** -0.5),
        'group_norm': gain(ks[18], (DEPTH, 2, GROUP_WIDTH)),
        'w_out': nrm(ks[19], (DEPTH, 2 * GROUP_WIDTH, D_MODEL), (2 * GROUP_WIDTH) ** -0.5),
        'norm_ffn': gain(ks[20], (DEPTH, D_MODEL)),
        'router_group_w': nrm(ks[21], (DEPTH, D_MODEL, N_GROUPS), D_MODEL ** -0.5),
        'router_group_b': nrm(ks[22], (DEPTH, N_GROUPS), 0.01),
        'router_expert_w': nrm(ks[23], (DEPTH, D_MODEL, N_EXPERTS), D_MODEL ** -0.5),
        'router_expert_b': nrm(ks[24], (DEPTH, N_EXPERTS), 0.01),
        'w_gate': nrm(ks[25], (DEPTH, N_EXPERTS, D_MODEL, D_EXPERT), D_MODEL ** -0.5),
        'w_up': nrm(ks[26], (DEPTH, N_EXPERTS, D_MODEL, D_EXPERT), D_MODEL ** -0.5),
        'w_down': nrm(ks[27], (DEPTH, N_EXPERTS, D_EXPERT, D_MODEL), D_EXPERT ** -0.5),
        'ple_proj': nrm(ks[28], (DEPTH, PLE_DIM, D_MODEL), PLE_DIM ** -0.5),
        'ple_norm': gain(ks[29], (DEPTH, D_MODEL)),
        'ple_gate_w': nrm(ks[30], (DEPTH, D_MODEL, D_MODEL), D_MODEL ** -0.5),
        'norm_final': gain(ks[31], (D_MODEL,)),
    }


def reference(x_prompt, x_sample, p_prompt, p_sample, cache_nsa_cmp, cache_nsa_sel, cache_mla, state_nsa_win,
              page_table, norm_attn, w_in, nsa_cmp_pos, nsa_cmp_w, mla_q_norm, mla_kv_norm, mla_w_uq, mla_w_uk,
              mla_w_uv, group_norm, w_out, norm_ffn, router_group_w, router_group_b, router_expert_w,
              router_expert_b, w_gate, w_up, w_down, ple_proj, ple_norm, ple_gate_w, norm_final):
    seq = x_prompt.shape[1]
    dec_seq = x_sample.shape[1]
    past = page_table.shape[1] * PAGE_SIZE
    pos_prompt = jnp.arange(seq)
    pos_sample = past + jnp.arange(dec_seq)
    hp, hs = x_prompt, x_sample
    cmp_p, cmp_s, sel_p, sel_s, mla_p, mla_s, win_p, win_s = [], [], [], [], [], [], [], []
    for i in range(DEPTH):
        def finish(h, o_nsa, o_lat, p_i):
            h = h + mixer_output(o_nsa, o_lat, mla_w_uv[i], group_norm[i], w_out[i], h.dtype)
            h = h + hier_moe(rms_norm(h, norm_ffn[i]), router_group_w[i], router_group_b[i],
                             router_expert_w[i], router_expert_b[i], w_gate[i], w_up[i], w_down[i])
            return ple_add(h, p_i, ple_proj[i], ple_norm[i], ple_gate_w[i])

        q, kv_c, kv_s, kv_w, gates, q_lat, q_pe, mla_row = mixer_inputs(
            rms_norm(hp, norm_attn[i]), pos_prompt, w_in[i], mla_q_norm[i], mla_kv_norm[i], mla_w_uq[i], mla_w_uk[i])
        o_nsa = nsa_prompt(q, kv_c, kv_s, kv_w, gates, nsa_cmp_pos[i], nsa_cmp_w[i])
        o_lat = mla_prompt(q_lat, q_pe, mla_row)
        hp = finish(hp, o_nsa, o_lat, p_prompt[i])
        cmp_p.append(kv_c)
        sel_p.append(kv_s)
        mla_p.append(mla_row)
        win_p.append(kv_w[:, seq - min(WINDOW, seq):])

        q, kv_c, kv_s, kv_w, gates, q_lat, q_pe, mla_row = mixer_inputs(
            rms_norm(hs, norm_attn[i]), pos_sample, w_in[i], mla_q_norm[i], mla_kv_norm[i], mla_w_uq[i], mla_w_uk[i])
        o_nsa, new_buf = nsa_sample(q, kv_c, kv_s, kv_w, gates, cache_nsa_cmp, cache_nsa_sel, state_nsa_win[i],
                                    page_table, i, nsa_cmp_pos[i], nsa_cmp_w[i])
        o_lat = mla_sample(q_lat, q_pe, mla_row, cache_mla, page_table, i)
        hs = finish(hs, o_nsa, o_lat, p_sample[i])
        cmp_s.append(kv_c)
        sel_s.append(kv_s)
        mla_s.append(mla_row)
        win_s.append(new_buf)

    y_prompt = rms_norm(hp, norm_final)
    y_sample = rms_norm(hs, norm_final)
    new_nsa_cmp_prompt = jnp.stack(cmp_p)
    new_nsa_cmp_sample = jnp.stack(cmp_s)
    new_nsa_sel_prompt = jnp.stack(sel_p)
    new_nsa_sel_sample = jnp.stack(sel_s)
    new_mla_prompt = jnp.stack(mla_p)
    new_mla_sample = jnp.stack(mla_s)
    new_win_prompt = jnp.stack(win_p)
    new_win_sample = jnp.stack(win_s)
    return (y_prompt, y_sample, new_nsa_cmp_prompt, new_nsa_cmp_sample, new_nsa_sel_prompt, new_nsa_sel_sample,
            new_mla_prompt, new_mla_sample, new_win_prompt, new_win_sample)
```

```python
import functools

import numpy as np
import jax
import jax.numpy as jnp
from jax import lax
from jax.experimental import pallas as pl
from jax.experimental.pallas import tpu as pltpu

F32 = jnp.float32
BF16 = jnp.bfloat16
I32 = jnp.int32

PAGE_SIZE = 128
HEAD_DIM = 128
NSA_HEADS = 16
CMP_BLOCK = 32
SEL_BLOCK = 64
N_SELECT = 16
WINDOW = 512
MLA_HEADS = 16
MLA_Q_LORA = 896
MLA_KV_LORA = 320
MLA_NOPE = 128
MLA_ROPE = 64
MLA_V = 128
MLA_ROW = MLA_KV_LORA + MLA_ROPE
N_GROUPS = 8
EXPERTS_PER_GROUP = 8
N_EXPERTS = N_GROUPS * EXPERTS_PER_GROUP
TOPK_IN_GROUP = 2
ROPE_THETA = 10000.0
RMS_EPS = 1e-6
QUERY_BLOCK = 128
FORCE_BONUS = 1e4
NEG_INF = -1e30
TINY = 1e-30
NSA_SCALE = HEAD_DIM ** -0.5
MLA_SCALE = (MLA_NOPE + MLA_ROPE) ** -0.5
NSA_Q_COLS = NSA_HEADS * HEAD_DIM
NSA_KV_COLS = 3 * 2 * HEAD_DIM
NSA_GATE_COLS = 3 * NSA_HEADS
GROUP_WIDTH = NSA_HEADS * HEAD_DIM

LANES = 128
VMEM_LIMIT = 56 * 1024 * 1024

ZC_Q = 0
ZC_KV = ZC_Q + NSA_Q_COLS
ZC_CQ = ZC_KV + NSA_KV_COLS
ZC_MLA = ZC_CQ + MLA_Q_LORA
ZC_GATE = ZC_MLA + MLA_ROW
Z_COLS = ZC_GATE + LANES

_CONTRACT_LAST = (((1,), (1,)), ((), ()))


def _cparams(sem, vmem=VMEM_LIMIT):
    return pltpu.CompilerParams(dimension_semantics=sem, vmem_limit_bytes=vmem)


def _rms(x, g):
    return x * lax.rsqrt(jnp.mean(x * x, axis=-1, keepdims=True) + RMS_EPS) * g


def _softmax_heads(s3, bias, maskf):
    s = s3 + bias[None]
    m = jnp.max(s, axis=-1, keepdims=True)
    e = jnp.exp(s - m) * maskf[None]
    return e / jnp.maximum(jnp.sum(e, axis=-1, keepdims=True), TINY)


def _norm_matmul_kernel(x_ref, g_ref, w_ref, o_ref, xn_ref):
    @pl.when(pl.program_id(1) == 0)
    def _():
        xn_ref[...] = _rms(x_ref[...], g_ref[...]).astype(BF16)

    o_ref[...] = jnp.dot(xn_ref[...], w_ref[...], preferred_element_type=F32)


def norm_matmul(x, g, w, tm, tn):
    n, d = x.shape
    c = w.shape[1]
    return pl.pallas_call(
        _norm_matmul_kernel,
        out_shape=jax.ShapeDtypeStruct((n, c), F32),
        grid=(n // tm, c // tn),
        in_specs=[pl.BlockSpec((tm, d), lambda i, j: (i, 0)),
                  pl.BlockSpec((1, d), lambda i, j: (0, 0)),
                  pl.BlockSpec((d, tn), lambda i, j: (0, j))],
        out_specs=pl.BlockSpec((tm, tn), lambda i, j: (i, j)),
        scratch_shapes=[pltpu.VMEM((tm, d), BF16)],
        compiler_params=_cparams(("parallel", "arbitrary")),
        name="norm_in_proj",
    )(x, g.reshape(1, d), w)


def _rope_half(x, cos, sin):
    return x * cos + pltpu.roll(x, 64, 1) * sin


def _rope_quarter(x, cos, sin):
    lane = lax.broadcasted_iota(I32, x.shape, 1)
    partner = jnp.where((lane % 64) < 32, pltpu.roll(x, 96, 1), pltpu.roll(x, 32, 1))
    return x * cos + partner * sin


def _split_kernel(z_ref, c128_ref, s128_ref, c64_ref, s64_ref, gq_ref, gkv_ref, wuq_ref, wuk_ref,
                  q_ref, kvc_ref, kvs_ref, kvw_ref, kvsb_ref, kvwb_ref, gate_ref, qcat_ref, row_ref, rowb_ref):
    c128, s128 = c128_ref[...], s128_ref[...]
    c64, s64 = c64_ref[...], s64_ref[...]
    for h in range(NSA_HEADS):
        x = z_ref[:, ZC_Q + h * 128:ZC_Q + (h + 1) * 128]
        q_ref[:, h * 128:(h + 1) * 128] = (_rope_half(x, c128, s128) * NSA_SCALE).astype(BF16)
    for br, (o32, o16) in enumerate(((kvc_ref, None), (kvs_ref, kvsb_ref), (kvw_ref, kvwb_ref))):
        c0 = ZC_KV + br * 256
        k = _rope_half(z_ref[:, c0:c0 + 128], c128, s128)
        v = z_ref[:, c0 + 128:c0 + 256]
        o32[:, 0:128] = k
        o32[:, 128:256] = v
        if o16 is not None:
            o16[:, 0:128] = k.astype(BF16)
            o16[:, 128:256] = v.astype(BF16)
    gate_ref[...] = jax.nn.sigmoid(z_ref[:, ZC_GATE:ZC_GATE + LANES])

    zc = z_ref[:, ZC_MLA:ZC_MLA + MLA_ROW]
    lane384 = lax.broadcasted_iota(I32, zc.shape, 1)
    sq = jnp.where(lane384 < MLA_KV_LORA, zc * zc, 0.0)
    rstd = lax.rsqrt(jnp.sum(sq, axis=-1, keepdims=True) / MLA_KV_LORA + RMS_EPS)
    normed = zc * rstd * gkv_ref[...]
    x3 = zc[:, 256:384]
    lane128 = lax.broadcasted_iota(I32, x3.shape, 1)
    col3 = jnp.where(lane128 < 64, normed[:, 256:384], _rope_quarter(x3, c64, s64))
    row_ref[:, 0:256] = normed[:, 0:256]
    row_ref[:, 256:384] = col3
    rowb_ref[:, 0:256] = normed[:, 0:256].astype(BF16)
    rowb_ref[:, 256:384] = col3.astype(BF16)

    cq = _rms(z_ref[:, ZC_CQ:ZC_CQ + MLA_Q_LORA], gq_ref[...]).astype(BF16)
    qm = jnp.dot(cq, wuq_ref[...], preferred_element_type=F32)
    pe_base = MLA_HEADS * MLA_NOPE
    for hp in range(MLA_HEADS // 2):
        pe = _rope_quarter(qm[:, pe_base + hp * 128:pe_base + (hp + 1) * 128], c64, s64)
        for sub in range(2):
            h = 2 * hp + sub
            nope = qm[:, h * 128:(h + 1) * 128].astype(BF16)
            ql = jnp.dot(nope, wuk_ref[h], preferred_element_type=F32)
            pe_hi = pe if sub == 1 else pltpu.roll(pe, 64, 1)
            tail = ql[:, 256:384] + jnp.where(lane128 >= 64, pe_hi, 0.0)
            qcat_ref[:, h * 384:h * 384 + 256] = (ql[:, 0:256] * MLA_SCALE).astype(BF16)
            qcat_ref[:, h * 384 + 256:(h + 1) * 384] = (tail * MLA_SCALE).astype(BF16)


def split_projection(z, tabs, tab_index, gq, gkv, wuq, wuk, tm):
    n = z.shape[0]
    c128, s128, c64, s64 = tabs
    row = lambda w: pl.BlockSpec((tm, w), lambda i: (i, 0))
    tab = pl.BlockSpec((tm, LANES), lambda i: (tab_index(i), 0))
    full = lambda a: pl.BlockSpec(a.shape, lambda i: (0,) * a.ndim)
    outs = [((n, NSA_Q_COLS), BF16), ((n, 256), F32), ((n, 256), F32), ((n, 256), F32), ((n, 256), BF16),
            ((n, 256), BF16), ((n, LANES), F32), ((n, MLA_HEADS * MLA_ROW), BF16), ((n, MLA_ROW), F32),
            ((n, MLA_ROW), BF16)]
    return pl.pallas_call(
        _split_kernel,
        out_shape=[jax.ShapeDtypeStruct(s, d) for s, d in outs],
        grid=(n // tm,),
        in_specs=[row(Z_COLS), tab, tab, tab, tab, full(gq), full(gkv), full(wuq), full(wuk)],
        out_specs=[row(s[1]) for s, _ in outs],
        compiler_params=_cparams(("parallel",)),
        name="split_projection",
    )(z, c128, s128, c64, s64, gq, gkv, wuq, wuk)


def _compress_kernel(x_ref, pos_ref, w_ref, o_ref):
    o_ref[...] = jnp.dot((x_ref[...] + pos_ref[...]).astype(BF16), w_ref[...], preferred_element_type=F32)


def compress_rows(x, pos, w, tm):
    m, k = x.shape
    return pl.pallas_call(
        _compress_kernel,
        out_shape=jax.ShapeDtypeStruct((m, 256), F32),
        grid=(m // tm,),
        in_specs=[pl.BlockSpec((tm, k), lambda i: (i, 0)),
                  pl.BlockSpec((1, k), lambda i: (0, 0)),
                  pl.BlockSpec((k, 256), lambda i: (0, 0))],
        out_specs=pl.BlockSpec((tm, 256), lambda i: (i, 0)),
        compiler_params=_cparams(("parallel",)),
        name="compress_prompt",
    )(x, pos, w)


def _nsa_prompt_kernel(q_ref, g_ref, ck_ref, ks_ref, kw_ref, o_ref, *, seq, kc):
    H, QB = NSA_HEADS, QUERY_BLOCK
    n_sel = seq // SEL_BLOCK
    half = ck_ref.shape[1] // 2
    qb = pl.program_id(1)
    start = qb * QB
    Q = jnp.concatenate([q_ref[:, h * 128:(h + 1) * 128] for h in range(H)], axis=0)
    qpos = start + lax.broadcasted_iota(I32, (QB, 1), 0)

    ck = ck_ref[0]
    s_c = lax.dot_general(Q, ck[:, :128].astype(BF16), _CONTRACT_LAST, preferred_element_type=F32)
    lane = lax.broadcasted_iota(I32, (1, 2 * half), 1)
    jj = lane % half
    blk = 2 * jj + (lane >= half).astype(I32)
    mask_c = (((blk + 1) * CMP_BLOCK - 1) <= qpos) & (jj < n_sel)
    p_c = _softmax_heads(s_c.reshape(H, QB, 2 * half), jnp.where(mask_c, 0.0, NEG_INF), mask_c.astype(F32))
    o_c = jnp.dot(p_c.reshape(H * QB, 2 * half).astype(BF16), ck[:, 128:].astype(BF16),
                  preferred_element_type=F32)

    imp = jnp.sum(p_c, axis=0)
    imp = imp[:, :half] + imp[:, half:]
    j = lax.broadcasted_iota(I32, (1, half), 1)
    cur = qpos // SEL_BLOCK
    valid = (j * SEL_BLOCK <= qpos) & (j < n_sel)
    forced = (j == 0) | (j == cur) | (j == cur - 1)
    score = jnp.where(valid, imp + jnp.where(forced, FORCE_BONUS, 0.0), -jnp.inf)
    rank = jnp.zeros((QB, half), F32)
    for i in range(n_sel):
        col = score[:, i:i + 1]
        ahead = (col > score) | ((col == score) & (j > i))
        rank = rank + ahead.astype(F32)
    chosen = ((rank < min(N_SELECT, n_sel)) & valid).astype(BF16)
    jcol = lax.broadcasted_iota(I32, (half, 1), 0)

    def sel_body(c, carry):
        m, l, acc = carry
        k0 = pl.multiple_of(c * kc, kc)
        kk = ks_ref[pl.ds(k0, kc), 0:128]
        vv = ks_ref[pl.ds(k0, kc), 128:256]
        s = lax.dot_general(Q, kk, _CONTRACT_LAST, preferred_element_type=F32).reshape(H, QB, kc)
        kidx = k0 + lax.broadcasted_iota(I32, (1, kc), 1)
        expand = ((kidx // SEL_BLOCK) == jcol).astype(BF16)
        picked = jnp.dot(chosen, expand, preferred_element_type=F32) > 0.5
        ok = picked & (kidx <= qpos)
        s = s + jnp.where(ok, 0.0, NEG_INF)[None]
        m_new = jnp.maximum(m, jnp.max(s, axis=-1, keepdims=True))
        alpha = jnp.exp(m - m_new)
        e = jnp.exp(s - m_new) * ok.astype(F32)[None]
        l = alpha * l + jnp.sum(e, axis=-1, keepdims=True)
        pv = jnp.dot(e.reshape(H * QB, kc).astype(BF16), vv, preferred_element_type=F32)
        acc = alpha.reshape(H * QB, 1) * acc + pv
        return m_new, l, acc

    n_chunks = (start + QB + kc - 1) // kc
    m0 = jnp.full((H, QB, 1), NEG_INF, F32)
    l0 = jnp.zeros((H, QB, 1), F32)
    a0 = jnp.zeros((H * QB, 128), F32)
    _, l_s, acc_s = lax.fori_loop(0, n_chunks, sel_body, (m0, l0, a0))
    o_s = acc_s / jnp.maximum(l_s, TINY).reshape(H * QB, 1)

    wk = WINDOW + QB
    w0 = pl.multiple_of(jnp.clip(start - WINDOW, 0, seq - wk), QB)
    kk = kw_ref[pl.ds(w0, wk), 0:128]
    vv = kw_ref[pl.ds(w0, wk), 128:256]
    s_w = lax.dot_general(Q, kk, _CONTRACT_LAST, preferred_element_type=F32).reshape(H, QB, wk)
    kidx = w0 + lax.broadcasted_iota(I32, (1, wk), 1)
    mask_w = (kidx <= qpos) & (kidx > qpos - WINDOW)
    p_w = _softmax_heads(s_w, jnp.where(mask_w, 0.0, NEG_INF), mask_w.astype(F32))
    o_w = jnp.dot(p_w.reshape(H * QB, wk).astype(BF16), vv, preferred_element_type=F32)

    g = g_ref[...]
    for h in range(H):
        r = slice(h * QB, (h + 1) * QB)
        o_ref[:, h * 128:(h + 1) * 128] = (g[:, h:h + 1] * o_c[r] + g[:, H + h:H + h + 1] * o_s[r]
                                           + g[:, 2 * H + h:2 * H + h + 1] * o_w[r])


def nsa_prompt(q, gates, ck, kvs_b, kvw_b, batch, seq, kc=512):
    nqb = seq // QUERY_BLOCK
    assert seq % kc == 0 and seq >= WINDOW + QUERY_BLOCK
    return pl.pallas_call(
        functools.partial(_nsa_prompt_kernel, seq=seq, kc=kc),
        out_shape=jax.ShapeDtypeStruct((batch * seq, GROUP_WIDTH), F32),
        grid=(batch, nqb),
        in_specs=[pl.BlockSpec((QUERY_BLOCK, GROUP_WIDTH), lambda b, i: (b * nqb + i, 0)),
                  pl.BlockSpec((QUERY_BLOCK, LANES), lambda b, i: (b * nqb + i, 0)),
                  pl.BlockSpec((1,) + ck.shape[1:], lambda b, i: (b, 0, 0)),
                  pl.BlockSpec((seq, 256), lambda b, i: (b, 0)),
                  pl.BlockSpec((seq, 256), lambda b, i: (b, 0))],
        out_specs=pl.BlockSpec((QUERY_BLOCK, GROUP_WIDTH), lambda b, i: (b * nqb + i, 0)),
        compiler_params=_cparams(("parallel", "parallel")),
        name="nsa_prompt",
    )(q, gates, ck, kvs_b, kvw_b)


def _mla_prompt_kernel(q_ref, rows_ref, o_ref, *, kc):
    H, QB = MLA_HEADS, QUERY_BLOCK
    start = pl.program_id(1) * QB
    Q = jnp.concatenate([q_ref[:, h * MLA_ROW:(h + 1) * MLA_ROW] for h in range(H)], axis=0)
    qpos = start + lax.broadcasted_iota(I32, (QB, 1), 0)

    def body(c, carry):
        m, l, acc = carry
        k0 = pl.multiple_of(c * kc, kc)
        rows = rows_ref[pl.ds(k0, kc), :]
        s = lax.dot_general(Q, rows, _CONTRACT_LAST, preferred_element_type=F32).reshape(H, QB, kc)
        ok = (k0 + lax.broadcasted_iota(I32, (1, kc), 1)) <= qpos
        s = s + jnp.where(ok, 0.0, NEG_INF)[None]
        m_new = jnp.maximum(m, jnp.max(s, axis=-1, keepdims=True))
        alpha = jnp.exp(m - m_new)
        e = jnp.exp(s - m_new) * ok.astype(F32)[None]
        l = alpha * l + jnp.sum(e, axis=-1, keepdims=True)
        pv = jnp.dot(e.reshape(H * QB, kc).astype(BF16), rows, preferred_element_type=F32)
        acc = alpha.reshape(H * QB, 1) * acc + pv
        return m_new, l, acc

    n_chunks = (start + QB + kc - 1) // kc
    m0 = jnp.full((H, QB, 1), NEG_INF, F32)
    l0 = jnp.zeros((H, QB, 1), F32)
    a0 = jnp.zeros((H * QB, MLA_ROW), F32)
    _, l, acc = lax.fori_loop(0, n_chunks, body, (m0, l0, a0))
    o = (acc / jnp.maximum(l, TINY).reshape(H * QB, 1)).astype(BF16)
    for h in range(H):
        o_ref[:, h * MLA_ROW:(h + 1) * MLA_ROW] = o[h * QB:(h + 1) * QB]


def mla_prompt(qcat, rows_b, batch, seq, kc=512):
    nqb = seq // QUERY_BLOCK
    width = MLA_HEADS * MLA_ROW
    return pl.pallas_call(
        functools.partial(_mla_prompt_kernel, kc=kc),
        out_shape=jax.ShapeDtypeStruct((batch * seq, width), BF16),
        grid=(batch, nqb),
        in_specs=[pl.BlockSpec((QUERY_BLOCK, width), lambda b, i: (b * nqb + i, 0)),
                  pl.BlockSpec((seq, MLA_ROW), lambda b, i: (b, 0))],
        out_specs=pl.BlockSpec((QUERY_BLOCK, width), lambda b, i: (b * nqb + i, 0)),
        compiler_params=_cparams(("parallel", "parallel")),
        name="mla_prompt",
    )(qcat, rows_b)


def _page_copy(cache_ref, page, buf, slot, p, rows, sem):
    return pltpu.make_async_copy(cache_ref.at[page], buf.at[slot, pl.ds(p * rows, rows), :], sem.at[slot])


def _gather_pipeline(pt_ref, cache_ref, buf, sem, ch, rows):
    b, c = pl.program_id(0), pl.program_id(1)
    nc = pl.num_programs(1)
    t = b * nc + c
    total = pl.num_programs(0) * nc
    slot = t % 2

    def start(tb, tc, sl):
        for p in range(ch):
            _page_copy(cache_ref, pt_ref[tb, tc * ch + p], buf, sl, p, rows, sem).start()

    @pl.when(t == 0)
    def _():
        start(0, 0, 0)

    @pl.when(t + 1 < total)
    def _():
        start((t + 1) // nc, (t + 1) % nc, 1 - slot)

    for p in range(ch):
        _page_copy(cache_ref, 0, buf, slot, p, rows, sem).wait()
    return slot


def _cmp_sample_kernel(pt_ref, q_ref, w_ref, pos_ref, cache_ref, oc_ref, imp_ref,
                       buf, sem, y_scr, ck_scr, *, ch, past, dec_seq):
    c = pl.program_id(1)
    slot = _gather_pipeline(pt_ref, cache_ref, buf, sem, ch, 8)
    x = (buf[slot].reshape(ch, 8, buf.shape[2]) + pos_ref[...][None]).reshape(8 * ch, buf.shape[2])
    y = jnp.dot(x.astype(BF16), w_ref[...], preferred_element_type=F32)
    for gidx in range(4):
        y_scr[gidx] = y[:, gidx * 128:(gidx + 1) * 128]
    r0 = pl.multiple_of(c * 2 * ch, 2 * ch)
    for parity in range(2):
        for kv in range(2):
            ck_scr[parity, kv, pl.ds(r0, 2 * ch), :] = (y_scr[kv, pl.ds(2 * parity, 2 * ch, stride=4), :]
                                                        + y_scr[2 + kv, pl.ds(2 * parity + 1, 2 * ch, stride=4), :])

    @pl.when(c == pl.num_programs(1) - 1)
    def _():
        H = NSA_HEADS
        nsel = ck_scr.shape[2]
        ckk = jnp.concatenate([ck_scr[0, 0], ck_scr[1, 0]], axis=0).astype(BF16)
        ckv = jnp.concatenate([ck_scr[0, 1], ck_scr[1, 1]], axis=0).astype(BF16)
        q = q_ref[0]
        s = lax.dot_general(q, ckk, _CONTRACT_LAST, preferred_element_type=F32)
        lane = lax.broadcasted_iota(I32, (1, 2 * nsel), 1)
        blk = 2 * (lane % nsel) + (lane >= nsel).astype(I32)
        qpos = past + lax.broadcasted_iota(I32, (dec_seq * H, 1), 0) // H
        mask = ((blk + 1) * CMP_BLOCK - 1) <= qpos
        s = jnp.where(mask, s, NEG_INF)
        m = jnp.max(s, axis=-1, keepdims=True)
        e = jnp.where(mask, jnp.exp(s - m), 0.0)
        p = e / jnp.maximum(jnp.sum(e, axis=-1, keepdims=True), TINY)
        oc_ref[0] = jnp.dot(p.astype(BF16), ckv, preferred_element_type=F32)
        imp = jnp.sum(p.reshape(dec_seq, H, 2 * nsel), axis=1)
        imp_ref[0] = imp[:, :nsel] + imp[:, nsel:]


def cmp_sample(page_table, q_s, w_halves, pos8, cache, dec_seq, ch):
    db, n_pages = page_table.shape
    nsel = 2 * n_pages
    rows = dec_seq * NSA_HEADS
    half_cols = cache.shape[2]
    grid_spec = pltpu.PrefetchScalarGridSpec(
        num_scalar_prefetch=1,
        grid=(db, n_pages // ch),
        in_specs=[pl.BlockSpec((1, rows, 128), lambda b, c, pt: (b, 0, 0)),
                  pl.BlockSpec(w_halves.shape, lambda b, c, pt: (0, 0)),
                  pl.BlockSpec(pos8.shape, lambda b, c, pt: (0, 0)),
                  pl.BlockSpec(memory_space=pl.ANY)],
        out_specs=[pl.BlockSpec((1, rows, 128), lambda b, c, pt: (b, 0, 0)),
                   pl.BlockSpec((1, dec_seq, nsel), lambda b, c, pt: (b, 0, 0))],
        scratch_shapes=[pltpu.VMEM((2, 8 * ch, half_cols), F32),
                        pltpu.SemaphoreType.DMA((2,)),
                        pltpu.VMEM((4, 8 * ch, 128), F32),
                        pltpu.VMEM((2, 2, nsel, 128), F32)])
    return pl.pallas_call(
        functools.partial(_cmp_sample_kernel, ch=ch, past=n_pages * PAGE_SIZE, dec_seq=dec_seq),
        out_shape=[jax.ShapeDtypeStruct((db, rows, 128), F32), jax.ShapeDtypeStruct((db, dec_seq, nsel), F32)],
        grid_spec=grid_spec,
        compiler_params=_cparams(("arbitrary", "arbitrary")),
        name="cmp_sample",
    )(page_table, q_s, w_halves, pos8, cache)


def _select_kernel(imp_ref, idx_ref, val_ref, *, past, dec_seq, n_sel):
    nq, n_past = imp_ref.shape
    x = jnp.concatenate([imp_ref[...], jnp.zeros((nq, LANES), F32)], axis=1)
    width = n_past + LANES
    j = lax.broadcasted_iota(I32, (1, width), 1)
    jf = j.astype(F32)
    qpos = past + lax.broadcasted_iota(I32, (nq, 1), 0) % dec_seq
    cur = qpos // SEL_BLOCK
    valid = (j * SEL_BLOCK <= qpos) & (j < n_sel)
    forced = (j == 0) | (j == cur) | (j == cur - 1)
    score = jnp.where(valid, x + jnp.where(forced, FORCE_BONUS, 0.0), -jnp.inf)
    out_lane = lax.broadcasted_iota(I32, (1, LANES), 1)
    idx = jnp.zeros((nq, LANES), F32)
    val = jnp.zeros((nq, LANES), F32)
    for r in range(min(N_SELECT, n_sel)):
        m = jnp.max(score, axis=-1, keepdims=True)
        ix = jnp.min(jnp.where(score == m, jf, float(width)), axis=-1, keepdims=True)
        idx = jnp.where(out_lane == r, ix, idx)
        val = jnp.where(out_lane == r, (m > -jnp.inf).astype(F32), val)
        score = jnp.where(jf == ix, -jnp.inf, score)
    idx_ref[...] = idx.astype(I32)
    val_ref[...] = val.astype(I32)


def select_sample(imp, past, dec_seq, n_sel):
    nq = imp.shape[0]
    return pl.pallas_call(
        functools.partial(_select_kernel, past=past, dec_seq=dec_seq, n_sel=n_sel),
        out_shape=[jax.ShapeDtypeStruct((nq, LANES), I32), jax.ShapeDtypeStruct((nq, LANES), I32)],
        compiler_params=_cparams(None),
        name="select_sample",
    )(imp)


def _sel_copy(cache_ref, new_ref, pt_ref, idx_ref, b, e, buf, slot, sem, n_pages):
    n_past_blk = 2 * n_pages
    ix = idx_ref[b, e]
    dst = buf.at[slot, e]
    is_new = ix >= n_past_blk
    page = pt_ref[b, jnp.minimum(ix // 2, n_pages - 1)]

    def new_copy():
        return pltpu.make_async_copy(new_ref.at[b], dst, sem.at[slot])

    def past_copy():
        return pltpu.make_async_copy(cache_ref.at[page, ix % 2], dst, sem.at[slot])

    return is_new, new_copy, past_copy


def _sel_sample_kernel(pt_ref, idx_ref, q_ref, ok_ref, oc_ref, g_ref, win_ref, wnew_ref, wmask_ref,
                       cache_ref, new_ref, o_ref, buf, sem, kw_scr, *, dec_seq, n_pages):
    H = NSA_HEADS
    nsl = dec_seq * N_SELECT
    b = pl.program_id(0)
    nb = pl.num_programs(0)
    slot = b % 2

    def start(tb, sl):
        for e in range(nsl):
            is_new, new_copy, past_copy = _sel_copy(cache_ref, new_ref, pt_ref, idx_ref, tb, e, buf, sl, sem, n_pages)

            @pl.when(is_new)
            def _():
                new_copy().start()

            @pl.when(jnp.logical_not(is_new))
            def _():
                past_copy().start()

    @pl.when(b == 0)
    def _():
        start(0, 0)

    @pl.when(b + 1 < nb)
    def _():
        start(b + 1, 1 - slot)

    for e in range(nsl):
        pltpu.make_async_copy(new_ref.at[0], buf.at[slot, e], sem.at[slot]).wait()

    q = q_ref[0]
    kv = buf[slot].reshape(dec_seq, N_SELECT * SEL_BLOCK, 256)
    q3 = q.reshape(dec_seq, H, 128)
    s = jnp.einsum('shd,skd->shk', q3, kv[:, :, :128].astype(BF16), preferred_element_type=F32)
    ok = ok_ref[0][:, None, :] > 0.5
    s = jnp.where(ok, s, NEG_INF)
    m = jnp.max(s, axis=-1, keepdims=True)
    e_ = jnp.where(ok, jnp.exp(s - m), 0.0)
    p = e_ / jnp.maximum(jnp.sum(e_, axis=-1, keepdims=True), TINY)
    o_s = jnp.einsum('shk,skd->shd', p.astype(BF16), kv[:, :, 128:].astype(BF16),
                     preferred_element_type=F32).reshape(dec_seq * H, 128)

    wb = win_ref.shape[1]
    kw_scr[0:wb, :] = win_ref[0]
    kw_scr[wb:, :] = jnp.zeros((kw_scr.shape[0] - wb, 256), F32)
    kw_scr[wb:wb + wnew_ref.shape[1], :] = wnew_ref[0]
    kw = kw_scr[...]
    s = lax.dot_general(q, kw[:, :128].astype(BF16), _CONTRACT_LAST, preferred_element_type=F32)
    okw = wmask_ref[...] > 0.5
    s = jnp.where(okw, s, NEG_INF)
    m = jnp.max(s, axis=-1, keepdims=True)
    e_ = jnp.where(okw, jnp.exp(s - m), 0.0)
    p = e_ / jnp.maximum(jnp.sum(e_, axis=-1, keepdims=True), TINY)
    o_w = jnp.dot(p.astype(BF16), kw[:, 128:].astype(BF16), preferred_element_type=F32)

    g = g_ref[0]
    o_ref[0] = g[:, 0:1] * oc_ref[0] + g[:, 1:2] * o_s + g[:, 2:3] * o_w


def sel_sample(page_table, idx, q_s, okmask, o_c, gates_rows, win, wnew, wmask, cache_sel, new_blocks, dec_seq):
    db, n_pages = page_table.shape
    rows = dec_seq * NSA_HEADS
    nkeys = N_SELECT * SEL_BLOCK
    wk = wmask.shape[1]
    b3 = lambda shape: pl.BlockSpec((1,) + shape, lambda b, pt, ix: (b, 0, 0))
    grid_spec = pltpu.PrefetchScalarGridSpec(
        num_scalar_prefetch=2,
        grid=(db,),
        in_specs=[b3((rows, 128)), b3((dec_seq, nkeys)), b3((rows, 128)), b3((rows, 3)),
                  b3(win.shape[1:]), b3(wnew.shape[1:]),
                  pl.BlockSpec(wmask.shape, lambda b, pt, ix: (0, 0)),
                  pl.BlockSpec(memory_space=pl.ANY), pl.BlockSpec(memory_space=pl.ANY)],
        out_specs=b3((rows, 128)),
        scratch_shapes=[pltpu.VMEM((2, dec_seq * N_SELECT, SEL_BLOCK, 256), F32),
                        pltpu.SemaphoreType.DMA((2,)),
                        pltpu.VMEM((wk, 256), F32)])
    return pl.pallas_call(
        functools.partial(_sel_sample_kernel, dec_seq=dec_seq, n_pages=n_pages),
        out_shape=jax.ShapeDtypeStruct((db, rows, 128), F32),
        grid_spec=grid_spec,
        compiler_params=_cparams(("arbitrary",)),
        name="sel_win_sample",
    )(page_table, idx, q_s, okmask, o_c, gates_rows, win, wnew, wmask, cache_sel, new_blocks)


def _mla_sample_kernel(pt_ref, q_ref, new_ref, cache_ref, o_ref, buf, sem, m_scr, l_scr, acc_scr, new_scr,
                       *, ch, dec_seq):
    H = MLA_HEADS
    c = pl.program_id(1)
    slot = _gather_pipeline(pt_ref, cache_ref, buf, sem, ch, PAGE_SIZE)
    q = q_ref[0]

    @pl.when(c == 0)
    def _():
        m_scr[...] = jnp.full(m_scr.shape, NEG_INF, F32)
        l_scr[...] = jnp.zeros(l_scr.shape, F32)
        acc_scr[...] = jnp.zeros(acc_scr.shape, F32)

    def update(rows, ok):
        s = lax.dot_general(q, rows, _CONTRACT_LAST, preferred_element_type=F32)
        if ok is not None:
            s = jnp.where(ok, s, NEG_INF)
        m_old = m_scr[...]
        m_new = jnp.maximum(m_old, jnp.max(s, axis=-1, keepdims=True))
        alpha = jnp.exp(m_old - m_new)
        e = jnp.exp(s - m_new)
        if ok is not None:
            e = jnp.where(ok, e, 0.0)
        l_scr[...] = alpha * l_scr[...] + jnp.sum(e, axis=-1, keepdims=True)
        acc_scr[...] = alpha * acc_scr[...] + jnp.dot(e.astype(BF16), rows, preferred_element_type=F32)
        m_scr[...] = m_new

    update(buf[slot].astype(BF16), None)

    @pl.when(c == pl.num_programs(1) - 1)
    def _():
        nr = new_ref.shape[1]
        new_scr[...] = jnp.zeros(new_scr.shape, BF16)
        new_scr[0:nr, :] = new_ref[0]
        t = lax.broadcasted_iota(I32, (1, new_scr.shape[0]), 1)
        srow = lax.broadcasted_iota(I32, (dec_seq * H, 1), 0) // H
        update(new_scr[...], (t <= srow) & (t < dec_seq))
        o_ref[0] = (acc_scr[...] / jnp.maximum(l_scr[...], TINY)).astype(BF16)


def mla_sample(page_table, qcat_s, new_rows, cache, dec_seq, ch):
    db, n_pages = page_table.shape
    rows = dec_seq * MLA_HEADS
    grid_spec = pltpu.PrefetchScalarGridSpec(
        num_scalar_prefetch=1,
        grid=(db, n_pages // ch),
        in_specs=[pl.BlockSpec((1, rows, MLA_ROW), lambda b, c, pt: (b, 0, 0)),
                  pl.BlockSpec((1,) + new_rows.shape[1:], lambda b, c, pt: (b, 0, 0)),
                  pl.BlockSpec(memory_space=pl.ANY)],
        out_specs=pl.BlockSpec((1, rows, MLA_ROW), lambda b, c, pt: (b, 0, 0)),
        scratch_shapes=[pltpu.VMEM((2, ch * PAGE_SIZE, MLA_ROW), F32),
                        pltpu.SemaphoreType.DMA((2,)),
                        pltpu.VMEM((rows, 1), F32), pltpu.VMEM((rows, 1), F32),
                        pltpu.VMEM((rows, MLA_ROW), F32), pltpu.VMEM((LANES, MLA_ROW), BF16)])
    return pl.pallas_call(
        functools.partial(_mla_sample_kernel, ch=ch, dec_seq=dec_seq),
        out_shape=jax.ShapeDtypeStruct((db, rows, MLA_ROW), BF16),
        grid_spec=grid_spec,
        compiler_params=_cparams(("arbitrary", "arbitrary")),
        name="mla_sample",
    )(page_table, qcat_s, new_rows, cache)


def _head_proj_kernel(x_ref, w_ref, o_ref):
    for h in range(MLA_HEADS):
        o_ref[:, h * MLA_V:(h + 1) * MLA_V] = jnp.dot(x_ref[:, h * MLA_ROW:(h + 1) * MLA_ROW], w_ref[h],
                                                      preferred_element_type=F32)


def head_proj(x, w, tm):
    n = x.shape[0]
    return pl.pallas_call(
        _head_proj_kernel,
        out_shape=jax.ShapeDtypeStruct((n, GROUP_WIDTH), F32),
        grid=(n // tm,),
        in_specs=[pl.BlockSpec((tm, x.shape[1]), lambda i: (i, 0)),
                  pl.BlockSpec(w.shape, lambda i: (0, 0, 0))],
        out_specs=pl.BlockSpec((tm, GROUP_WIDTH), lambda i: (i, 0)),
        compiler_params=_cparams(("parallel",)),
        name="mla_value_proj",
    )(x, w)


def _out_proj_kernel(a_ref, b_ref, ga_ref, gb_ref, w_ref, h_ref, o_ref, cat_ref):
    @pl.when(pl.program_id(1) == 0)
    def _():
        cat_ref[:, 0:GROUP_WIDTH] = _rms(a_ref[...], ga_ref[...]).astype(BF16)
        cat_ref[:, GROUP_WIDTH:2 * GROUP_WIDTH] = _rms(b_ref[...], gb_ref[...]).astype(BF16)

    o_ref[...] = h_ref[...] + jnp.dot(cat_ref[...], w_ref[...], preferred_element_type=F32)


def out_proj(o_nsa, o_mla, ga, gb, w, h, tm, tn):
    n, d = h.shape
    gw = GROUP_WIDTH
    return pl.pallas_call(
        _out_proj_kernel,
        out_shape=jax.ShapeDtypeStruct((n, d), F32),
        grid=(n // tm, d // tn),
        in_specs=[pl.BlockSpec((tm, gw), lambda i, j: (i, 0)),
                  pl.BlockSpec((tm, gw), lambda i, j: (i, 0)),
                  pl.BlockSpec((1, gw), lambda i, j: (0, 0)),
                  pl.BlockSpec((1, gw), lambda i, j: (0, 0)),
                  pl.BlockSpec((2 * gw, tn), lambda i, j: (0, j)),
                  pl.BlockSpec((tm, tn), lambda i, j: (i, j))],
        out_specs=pl.BlockSpec((tm, tn), lambda i, j: (i, j)),
        scratch_shapes=[pltpu.VMEM((tm, 2 * gw), BF16)],
        compiler_params=_cparams(("parallel", "arbitrary")),
        name="out_proj",
    )(o_nsa, o_mla, ga, gb, w, h)


def _router_kernel(h_ref, g_ref, w_ref, b_ref, xn_ref, ids_ref, gts_ref):
    xn = _rms(h_ref[...], g_ref[...])
    xn_ref[...] = xn.astype(BF16)
    logits = jnp.dot(xn, w_ref[...], preferred_element_type=F32, precision=lax.Precision.HIGHEST) + b_ref[...]
    lane = lax.broadcasted_iota(I32, logits.shape, 1)
    lanef = lane.astype(F32)
    big = float(LANES)

    def first_argmax(v):
        m = jnp.max(v, axis=-1, keepdims=True)
        return m, jnp.min(jnp.where(v == m, lanef, big), axis=-1, keepdims=True)

    gl = jnp.where(lane < N_GROUPS, logits, -jnp.inf)
    gmax, gidx = first_argmax(gl)
    g_gate = 1.0 / jnp.sum(jnp.exp(gl - gmax), axis=-1, keepdims=True)
    group_of_lane = ((lane - N_GROUPS) // EXPERTS_PER_GROUP).astype(F32)
    in_group = (lane >= N_GROUPS) & (lane < N_GROUPS + N_EXPERTS) & (group_of_lane == gidx)
    el = jnp.where(in_group, logits, -jnp.inf)
    v0, i0 = first_argmax(el)
    v1, i1 = first_argmax(jnp.where(lanef == i0, -jnp.inf, el))
    e0 = jnp.exp(v0 - v0)
    e1 = jnp.exp(v1 - v0)
    den = e0 + e1
    ids = jnp.where(lane == 0, i0 - N_GROUPS, jnp.where(lane == 1, i1 - N_GROUPS, 0.0))
    gts = jnp.where(lane == 0, g_gate * (e0 / den), jnp.where(lane == 1, g_gate * (e1 / den), 0.0))
    ids_ref[...] = ids.astype(I32)
    gts_ref[...] = gts


def router(h, g, w, b, tm):
    n, d = h.shape
    return pl.pallas_call(
        _router_kernel,
        out_shape=[jax.ShapeDtypeStruct((n, d), BF16), jax.ShapeDtypeStruct((n, LANES), I32),
                   jax.ShapeDtypeStruct((n, LANES), F32)],
        grid=(n // tm,),
        in_specs=[pl.BlockSpec((tm, d), lambda i: (i, 0)),
                  pl.BlockSpec((1, d), lambda i: (0, 0)),
                  pl.BlockSpec((d, LANES), lambda i: (0, 0)),
                  pl.BlockSpec((1, LANES), lambda i: (0, 0))],
        out_specs=[pl.BlockSpec((tm, d), lambda i: (i, 0)),
                   pl.BlockSpec((tm, LANES), lambda i: (i, 0)),
                   pl.BlockSpec((tm, LANES), lambda i: (i, 0))],
        compiler_params=_cparams(("parallel",)),
        name="ffn_norm_router",
    )(h, g.reshape(1, d), w, b)


def _ffn_kernel(be_ref, nu_ref, x_ref, wg_ref, wu_ref, wd_ref, o_ref):
    i, f = pl.program_id(0), pl.program_id(1)

    @pl.when(i < nu_ref[0])
    def _():
        x = x_ref[...]
        gate = jnp.dot(x, wg_ref[...].astype(BF16), preferred_element_type=F32)
        up = jnp.dot(x, wu_ref[...].astype(BF16), preferred_element_type=F32)
        hmid = (jax.nn.silu(gate) * up).astype(BF16)
        y = jnp.dot(hmid, wd_ref[...].astype(BF16), preferred_element_type=F32)

        @pl.when(f == 0)
        def _():
            o_ref[...] = y

        @pl.when(f > 0)
        def _():
            o_ref[...] += y

    @pl.when((i >= nu_ref[0]) & (f == 0))
    def _():
        o_ref[...] = jnp.zeros(o_ref.shape, F32)


def expert_ffn(block_expert, n_used, xs, w_gate, w_up, w_down, tm, tf):
    rows, d = xs.shape
    d_exp = w_gate.shape[2]
    nf = d_exp // tf

    def blk(i, nu):
        return jnp.minimum(i, nu[0] - 1)

    def fidx(i, f, nu):
        return jnp.where(i < nu[0], f, nf - 1)

    grid_spec = pltpu.PrefetchScalarGridSpec(
        num_scalar_prefetch=2,
        grid=(rows // tm, nf),
        in_specs=[pl.BlockSpec((tm, d), lambda i, f, be, nu: (blk(i, nu), 0)),
                  pl.BlockSpec((None, d, tf), lambda i, f, be, nu: (be[blk(i, nu)], 0, fidx(i, f, nu))),
                  pl.BlockSpec((None, d, tf), lambda i, f, be, nu: (be[blk(i, nu)], 0, fidx(i, f, nu))),
                  pl.BlockSpec((None, tf, d), lambda i, f, be, nu: (be[blk(i, nu)], fidx(i, f, nu), 0))],
        out_specs=pl.BlockSpec((tm, d), lambda i, f, be, nu: (i, 0)))
    return pl.pallas_call(
        _ffn_kernel,
        out_shape=jax.ShapeDtypeStruct((rows, d), F32),
        grid_spec=grid_spec,
        compiler_params=_cparams(("arbitrary", "arbitrary")),
        name="expert_ffn",
    )(block_expert, n_used, xs, w_gate, w_up, w_down)


def _ple_kernel(h_ref, hc_ref, g_ref, wg_ref, p_ref, wp_ref, o_ref, hn_ref):
    @pl.when(pl.program_id(1) == 0)
    def _():
        hn_ref[...] = _rms(h_ref[...], g_ref[...]).astype(BF16)

    gate = jax.nn.sigmoid(jnp.dot(hn_ref[...], wg_ref[...], preferred_element_type=F32))
    proj = jnp.dot(p_ref[...], wp_ref[...], preferred_element_type=F32)
    o_ref[...] = hc_ref[...] + gate * proj


def ple_add(h, g, wg, p, wp, tm, tn):
    n, d = h.shape
    pd = p.shape[1]
    return pl.pallas_call(
        _ple_kernel,
        out_shape=jax.ShapeDtypeStruct((n, d), F32),
        grid=(n // tm, d // tn),
        in_specs=[pl.BlockSpec((tm, d), lambda i, j: (i, 0)),
                  pl.BlockSpec((tm, tn), lambda i, j: (i, j)),
                  pl.BlockSpec((1, d), lambda i, j: (0, 0)),
                  pl.BlockSpec((d, tn), lambda i, j: (0, j)),
                  pl.BlockSpec((tm, pd), lambda i, j: (i, 0)),
                  pl.BlockSpec((pd, tn), lambda i, j: (0, j))],
        out_specs=pl.BlockSpec((tm, tn), lambda i, j: (i, j)),
        scratch_shapes=[pltpu.VMEM((tm, d), BF16)],
        compiler_params=_cparams(("parallel", "arbitrary")),
        name="ple_add",
    )(h, h, g.reshape(1, d), wg, p, wp)


def _final_norm_kernel(x_ref, g_ref, o_ref):
    o_ref[...] = _rms(x_ref[...], g_ref[...])


def final_norm(x, g, tm):
    n, d = x.shape
    return pl.pallas_call(
        _final_norm_kernel,
        out_shape=jax.ShapeDtypeStruct((n, d), F32),
        grid=(n // tm,),
        in_specs=[pl.BlockSpec((tm, d), lambda i: (i, 0)), pl.BlockSpec((1, d), lambda i: (0, 0))],
        out_specs=pl.BlockSpec((tm, d), lambda i: (i, 0)),
        compiler_params=_cparams(("parallel",)),
        name="final_norm",
    )(x, g.reshape(1, d))


def _rope_tables(pos, tm):
    pos = np.asarray(pos, np.float32)[:, None]

    def table(dim):
        inv = np.float32(ROPE_THETA) ** (-np.arange(0, dim, 2, dtype=np.float32) / np.float32(dim))
        ang = (pos * inv[None, :]).astype(np.float32).astype(np.float64)
        cos, sin = np.cos(ang).astype(np.float32), np.sin(ang).astype(np.float32)
        reps = LANES // dim
        return np.tile(np.concatenate([cos, cos], 1), (1, reps)), np.tile(np.concatenate([-sin, sin], 1), (1, reps))

    c128, s128 = table(HEAD_DIM)
    c64, s64 = table(MLA_ROPE)
    return tuple(jnp.asarray(t) for t in (c128, s128, c64, s64))


def _compress_weights(cmp_pos, cmp_w):
    zeros = jnp.zeros_like(cmp_w[0])
    wk = jnp.stack([cmp_w[0], zeros], axis=1)
    wv = jnp.stack([zeros, cmp_w[1]], axis=1)
    w = jnp.concatenate([wk, wv], axis=-1)
    w = w.reshape(CMP_BLOCK * 2 * HEAD_DIM, 2 * HEAD_DIM).astype(BF16)
    pos = jnp.transpose(cmp_pos, (1, 0, 2)).reshape(1, CMP_BLOCK * 2 * HEAD_DIM)
    return w, pos


def _layer(hp, hs, p_all, cache_cmp, cache_sel, cache_mla, win_state, page_table, wts):
    (norm_attn, w_in, cmp_pos, cmp_w, mla_q_norm, mla_kv_norm, mla_w_uq, mla_w_uk, mla_w_uv, group_norm, w_out,
     norm_ffn, rg_w, rg_b, re_w, re_b, w_gate, w_up, w_down, ple_proj, ple_norm, ple_gate_w) = wts
    batch, seq, d_model = hp.shape
    db, dec_seq, _ = hs.shape
    n_pages = page_table.shape[1]
    past = n_pages * PAGE_SIZE
    n_p, n_s = batch * seq, db * dec_seq
    n = n_p + n_s
    tm = int(np.gcd(np.gcd(512, n_s), seq))
    assert n_p % tm == 0 and n_s % tm == 0 and seq % tm == 0 and tm % dec_seq == 0
    assert dec_seq < CMP_BLOCK and dec_seq <= SEL_BLOCK and past % SEL_BLOCK == 0 and past >= WINDOW
    assert win_state.shape[1] == WINDOW

    h = jnp.concatenate([hp.reshape(n_p, d_model), hs.reshape(n_s, d_model)], axis=0)

    c0, c1 = NSA_Q_COLS, NSA_Q_COLS + NSA_KV_COLS
    c2 = c1 + NSA_GATE_COLS
    w_in_p = jnp.concatenate([w_in[:, :c1], w_in[:, c2:], w_in[:, c1:c2],
                              jnp.zeros((d_model, LANES - NSA_GATE_COLS), w_in.dtype)], axis=1).astype(BF16)
    z = norm_matmul(h, norm_attn, w_in_p, tm, 384)

    tm2 = min(256, tm)
    pos_rows = np.concatenate([np.arange(seq), past + np.arange(tm2) % dec_seq])
    tabs = _rope_tables(pos_rows, tm2)
    n_ptiles, seq_tiles = n_p // tm2, seq // tm2
    tab_index = lambda i: jnp.where(i < n_ptiles, i % seq_tiles, seq_tiles)
    wuq = jnp.concatenate([mla_w_uq[:, :, :MLA_NOPE].reshape(MLA_Q_LORA, -1),
                           mla_w_uq[:, :, MLA_NOPE:].reshape(MLA_Q_LORA, -1)], axis=1).astype(BF16)
    wuk = jnp.pad(jnp.transpose(mla_w_uk, (1, 2, 0)), ((0, 0), (0, 0), (0, MLA_ROPE))).astype(BF16)
    gkv = jnp.pad(mla_kv_norm, (0, MLA_ROPE)).reshape(1, MLA_ROW)
    (q_nsa, kv_c, kv_s, kv_w, kvs_b, kvw_b, gates, qcat, mla_row, mla_row_b) = split_projection(
        z, tabs, tab_index, mla_q_norm.reshape(1, -1), gkv, wuq, wuk, tm2)

    w_cmp, pos_cmp = _compress_weights(cmp_pos, cmp_w)
    blk_cols = CMP_BLOCK * 2 * HEAD_DIM
    nb = seq // CMP_BLOCK
    ck = compress_rows(kv_c[:n_p].reshape(n_p // CMP_BLOCK, blk_cols), pos_cmp, w_cmp, min(256, n_p // CMP_BLOCK))
    half = -(-(nb // 2) // LANES) * LANES
    ck = ck.reshape(batch, nb // 2, 2, 256)
    padh = ((0, 0), (0, half - nb // 2), (0, 0))
    ck = jnp.concatenate([jnp.pad(ck[:, :, 0], padh), jnp.pad(ck[:, :, 1], padh)], axis=1)
    o_nsa_p = nsa_prompt(q_nsa, gates, ck, kvs_b, kvw_b, batch, seq)

    o_lat_p = mla_prompt(qcat, mla_row_b, batch, seq)

    rows = dec_seq * NSA_HEADS
    q_s = q_nsa[n_p:].reshape(db, rows, HEAD_DIM)
    half_cols = blk_cols // 2
    n_pool = cache_cmp.shape[0]
    ch_c = min(64, n_pages)
    w_halves = jnp.concatenate([w_cmp[:half_cols], w_cmp[half_cols:]], axis=1)
    pos8 = jnp.tile(jnp.concatenate([pos_cmp[:, :half_cols], pos_cmp[:, half_cols:]], axis=0), (4, 1))
    o_c, imp = cmp_sample(page_table, q_s, w_halves, pos8, cache_cmp.reshape(n_pool, 8, half_cols), dec_seq, ch_c)
    n_sel = -(-(past + dec_seq) // SEL_BLOCK)
    n_past_blk = past // SEL_BLOCK
    idx_pad, val_pad = select_sample(imp.reshape(n_s, 2 * n_pages), past, dec_seq, n_sel)
    idx = idx_pad[:, :N_SELECT]
    val = val_pad[:, :N_SELECT] > 0
    qpos_s = past + jnp.arange(n_s, dtype=I32) % dec_seq
    kpos = idx[:, :, None] * SEL_BLOCK + jnp.arange(SEL_BLOCK, dtype=I32)
    okmask = (val[:, :, None] & (kpos <= qpos_s[:, None, None])).astype(F32).reshape(db, dec_seq, N_SELECT * SEL_BLOCK)
    kvs_new = kv_s[n_p:].reshape(db, dec_seq, 256)
    new_blocks = jnp.pad(kvs_new, ((0, 0), (0, SEL_BLOCK - dec_seq), (0, 0)))
    kvw_new = kv_w[n_p:].reshape(db, dec_seq, 256)
    wnew = jnp.pad(kvw_new, ((0, 0), (0, 8 - dec_seq % 8 if dec_seq % 8 else 0), (0, 0)))
    wk = WINDOW + LANES
    kp = np.concatenate([past - WINDOW + np.arange(WINDOW), past + np.arange(dec_seq),
                         np.full(wk - WINDOW - dec_seq, -1)])
    qp = past + np.arange(rows) // NSA_HEADS
    wmask = ((kp[None, :] <= qp[:, None]) & (kp[None, :] > qp[:, None] - WINDOW) & (kp[None, :] >= 0))
    wmask = jnp.asarray(wmask.astype(np.float32))
    g_s = gates[n_p:, :NSA_GATE_COLS].reshape(db, dec_seq, 3, NSA_HEADS)
    g_rows = jnp.transpose(g_s, (0, 1, 3, 2)).reshape(db, rows, 3)
    win = win_state.reshape(db, WINDOW, 256)
    o_nsa_s = sel_sample(page_table, idx.reshape(db, dec_seq * N_SELECT), q_s, okmask, o_c, g_rows, win, wnew, wmask,
                         cache_sel.reshape(n_pool, 2, SEL_BLOCK, 256), new_blocks, dec_seq)
    new_win = jnp.concatenate([win, kvw_new], axis=1)[:, dec_seq:]

    qcat_s = qcat[n_p:].reshape(db, dec_seq * MLA_HEADS, MLA_ROW)
    mla_new = jnp.pad(mla_row_b[n_p:].reshape(db, dec_seq, MLA_ROW), ((0, 0), (0, 16 - dec_seq), (0, 0)))
    o_lat_s = mla_sample(page_table, qcat_s, mla_new, cache_mla, dec_seq, min(16, n_pages))

    o_nsa = jnp.concatenate([o_nsa_p, o_nsa_s.reshape(n_s, GROUP_WIDTH)], axis=0)
    o_lat = jnp.concatenate([o_lat_p, o_lat_s.reshape(n_s, MLA_HEADS * MLA_ROW)], axis=0)
    wuv = jnp.pad(jnp.transpose(mla_w_uv, (1, 0, 2)), ((0, 0), (0, MLA_ROPE), (0, 0))).astype(BF16)
    o_mla = head_proj(o_lat, wuv, tm)
    h1 = out_proj(o_nsa, o_mla, group_norm[0:1], group_norm[1:2], w_out.astype(BF16), h, tm, 512)

    w_r = jnp.concatenate([rg_w, re_w, jnp.zeros((d_model, LANES - N_GROUPS - N_EXPERTS), F32)], axis=1)
    b_r = jnp.concatenate([rg_b, re_b, jnp.zeros((LANES - N_GROUPS - N_EXPERTS,), F32)]).reshape(1, LANES)
    xn, ids, gts = router(h1, norm_ffn, w_r, b_r, tm2)
    tmf = 256
    flat_e = ids[:, :TOPK_IN_GROUP].reshape(-1)
    nk = flat_e.shape[0]
    onehot = (flat_e[:, None] == jnp.arange(N_EXPERTS, dtype=I32)[None, :]).astype(I32)
    within = jnp.sum((jnp.cumsum(onehot, axis=0) - onehot) * onehot, axis=1)
    counts = jnp.sum(onehot, axis=0)
    padded = (counts + tmf - 1) // tmf * tmf
    pad_ends = jnp.cumsum(padded)
    dest = (pad_ends - padded)[flat_e] + within
    n_blocks = -(-(nk + N_EXPERTS * (tmf - 1)) // tmf)
    src = jnp.full((n_blocks * tmf,), n, I32).at[dest].set(jnp.arange(nk, dtype=I32) // TOPK_IN_GROUP)
    xs = jnp.concatenate([xn, jnp.zeros((1, d_model), BF16)], axis=0)[src]
    n_used = (pad_ends[-1] // tmf).astype(I32).reshape(1)
    block_expert = jnp.minimum(jnp.searchsorted(pad_ends, jnp.arange(n_blocks, dtype=I32) * tmf, side='right'),
                               N_EXPERTS - 1).astype(I32)
    ys = expert_ffn(block_expert, n_used, xs, w_gate, w_up, w_down, tmf, 256)
    yk = ys[dest].reshape(n, TOPK_IN_GROUP, d_model) * gts[:, :TOPK_IN_GROUP, None]
    h2 = h1 + (yk[:, 0] + yk[:, 1])

    h3 = ple_add(h2, ple_norm, ple_gate_w.astype(BF16), p_all.astype(BF16), ple_proj.astype(BF16), tm, 512)

    caches = dict(
        cmp_p=kv_c[:n_p].reshape(batch, seq, 2, HEAD_DIM), cmp_s=kv_c[n_p:].reshape(db, dec_seq, 2, HEAD_DIM),
        sel_p=kv_s[:n_p].reshape(batch, seq, 2, HEAD_DIM), sel_s=kv_s[n_p:].reshape(db, dec_seq, 2, HEAD_DIM),
        mla_p=mla_row[:n_p].reshape(batch, seq, MLA_ROW), mla_s=mla_row[n_p:].reshape(db, dec_seq, MLA_ROW),
        win_p=kv_w[:n_p].reshape(batch, seq, 2, HEAD_DIM)[:, seq - min(WINDOW, seq):],
        win_s=new_win.reshape(db, WINDOW, 2, HEAD_DIM))
    return h3[:n_p].reshape(batch, seq, d_model), h3[n_p:].reshape(db, dec_seq, d_model), caches


def kernel(x_prompt, x_sample, p_prompt, p_sample, cache_nsa_cmp, cache_nsa_sel, cache_mla, state_nsa_win, page_table, norm_attn, w_in, nsa_cmp_pos, nsa_cmp_w, mla_q_norm, mla_kv_norm, mla_w_uq, mla_w_uk, mla_w_uv, group_norm, w_out, norm_ffn, router_group_w, router_group_b, router_expert_w, router_expert_b, w_gate, w_up, w_down, ple_proj, ple_norm, ple_gate_w, norm_final):
    depth = w_in.shape[0]
    hp, hs = x_prompt, x_sample
    per_layer = []
    for i in range(depth):
        ple_dim = p_prompt.shape[-1]
        p_all = jnp.concatenate([p_prompt[i].reshape(-1, ple_dim), p_sample[i].reshape(-1, ple_dim)], axis=0)
        wts = (norm_attn[i], w_in[i], nsa_cmp_pos[i], nsa_cmp_w[i], mla_q_norm[i], mla_kv_norm[i], mla_w_uq[i],
               mla_w_uk[i], mla_w_uv[i], group_norm[i], w_out[i], norm_ffn[i], router_group_w[i], router_group_b[i],
               router_expert_w[i], router_expert_b[i], w_gate[i], w_up[i], w_down[i], ple_proj[i], ple_norm[i],
               ple_gate_w[i])
        hp, hs, caches = _layer(hp, hs, p_all, cache_nsa_cmp[i], cache_nsa_sel[i], cache_mla[i], state_nsa_win[i],
                                page_table, wts)
        per_layer.append(caches)
    d_model = hp.shape[-1]
    n_p = hp.shape[0] * hp.shape[1]
    n_s = hs.shape[0] * hs.shape[1]
    y = final_norm(jnp.concatenate([hp.reshape(n_p, d_model), hs.reshape(n_s, d_model)], axis=0), norm_final,
                   int(np.gcd(np.gcd(512, n_s), n_p)))
    y_prompt = y[:n_p].reshape(hp.shape)
    y_sample = y[n_p:].reshape(hs.shape)
    stack = lambda k: jnp.stack([c[k] for c in per_layer])
    return (y_prompt, y_sample, stack('cmp_p'), stack('cmp_s'), stack('sel_p'), stack('sel_s'),
            stack('mla_p'), stack('mla_s'), stack('win_p'), stack('win_s'))
```

```python
import functools

import numpy as np
import jax
import jax.numpy as jnp
from jax import lax
from jax.experimental import pallas as pl
from jax.experimental.pallas import tpu as pltpu

F32 = jnp.float32
BF16 = jnp.bfloat16
I32 = jnp.int32

PAGE_SIZE = 128
HEAD_DIM = 128
NSA_HEADS = 16
CMP_BLOCK = 32
SEL_BLOCK = 64
N_SELECT = 16
WINDOW = 512
MLA_HEADS = 16
MLA_Q_LORA = 896
MLA_KV_LORA = 320
MLA_NOPE = 128
MLA_ROPE = 64
MLA_V = 128
MLA_ROW = MLA_KV_LORA + MLA_ROPE
N_GROUPS = 8
EXPERTS_PER_GROUP = 8
N_EXPERTS = N_GROUPS * EXPERTS_PER_GROUP
TOPK_IN_GROUP = 2
ROPE_THETA = 10000.0
RMS_EPS = 1e-6
QUERY_BLOCK = 128
FORCE_BONUS = 1e4
NEG_INF = -1e30
TINY = 1e-30
NSA_SCALE = HEAD_DIM ** -0.5
MLA_SCALE = (MLA_NOPE + MLA_ROPE) ** -0.5
NSA_Q_COLS = NSA_HEADS * HEAD_DIM
NSA_KV_COLS = 3 * 2 * HEAD_DIM
NSA_GATE_COLS = 3 * NSA_HEADS
GROUP_WIDTH = NSA_HEADS * HEAD_DIM

LANES = 128
VMEM_LIMIT = 56 * 1024 * 1024

ZC_Q = 0
ZC_KV = ZC_Q + NSA_Q_COLS
ZC_CQ = ZC_KV + NSA_KV_COLS
ZC_MLA = ZC_CQ + MLA_Q_LORA
ZC_GATE = ZC_MLA + MLA_ROW
Z_COLS = ZC_GATE + LANES

_CONTRACT_LAST = (((1,), (1,)), ((), ()))


def _cparams(sem, vmem=VMEM_LIMIT):
    return pltpu.CompilerParams(dimension_semantics=sem, vmem_limit_bytes=vmem)


def _rms(x, g):
    return x * lax.rsqrt(jnp.mean(x * x, axis=-1, keepdims=True) + RMS_EPS) * g


def _softmax_heads(s3, bias, maskf):
    s = s3 + bias[None]
    m = jnp.max(s, axis=-1, keepdims=True)
    e = jnp.exp(s - m) * maskf[None]
    return e / jnp.maximum(jnp.sum(e, axis=-1, keepdims=True), TINY)


def _norm_matmul_kernel(x_ref, g_ref, w_ref, o_ref, xn_ref):
    @pl.when(pl.program_id(1) == 0)
    def _():
        xn_ref[...] = _rms(x_ref[...], g_ref[...]).astype(BF16)

    o_ref[...] = jnp.dot(xn_ref[...], w_ref[...], preferred_element_type=F32)


def norm_matmul(x, g, w, tm, tn):
    n, d = x.shape
    c = w.shape[1]
    return pl.pallas_call(
        _norm_matmul_kernel,
        out_shape=jax.ShapeDtypeStruct((n, c), F32),
        grid=(n // tm, c // tn),
        in_specs=[pl.BlockSpec((tm, d), lambda i, j: (i, 0)),
                  pl.BlockSpec((1, d), lambda i, j: (0, 0)),
                  pl.BlockSpec((d, tn), lambda i, j: (0, j))],
        out_specs=pl.BlockSpec((tm, tn), lambda i, j: (i, j)),
        scratch_shapes=[pltpu.VMEM((tm, d), BF16)],
        compiler_params=_cparams(("parallel", "arbitrary")),
        name="norm_in_proj",
    )(x, g.reshape(1, d), w)


def _rope_half(x, cos, sin):
    return x * cos + pltpu.roll(x, 64, 1) * sin


def _rope_quarter(x, cos, sin):
    lane = lax.broadcasted_iota(I32, x.shape, 1)
    partner = jnp.where((lane % 64) < 32, pltpu.roll(x, 96, 1), pltpu.roll(x, 32, 1))
    return x * cos + partner * sin


def _split_kernel(z_ref, c128_ref, s128_ref, c64_ref, s64_ref, gq_ref, gkv_ref, wuq_ref, wuk_ref,
                  q_ref, kvc_ref, kvs_ref, kvw_ref, kvsb_ref, kvwb_ref, gate_ref, qcat_ref, row_ref, rowb_ref):
    c128, s128 = c128_ref[...], s128_ref[...]
    c64, s64 = c64_ref[...], s64_ref[...]
    for h in range(NSA_HEADS):
        x = z_ref[:, ZC_Q + h * 128:ZC_Q + (h + 1) * 128]
        q_ref[:, h * 128:(h + 1) * 128] = (_rope_half(x, c128, s128) * NSA_SCALE).astype(BF16)
    for br, (o32, o16) in enumerate(((kvc_ref, None), (kvs_ref, kvsb_ref), (kvw_ref, kvwb_ref))):
        c0 = ZC_KV + br * 256
        k = _rope_half(z_ref[:, c0:c0 + 128], c128, s128)
        v = z_ref[:, c0 + 128:c0 + 256]
        o32[pl.ds(0, k.shape[0], stride=2), :] = k
        o32[pl.ds(1, k.shape[0], stride=2), :] = v
        if o16 is not None:
            o16[:, 0:128] = k.astype(BF16)
            o16[:, 128:256] = v.astype(BF16)
    gate_ref[...] = jax.nn.sigmoid(z_ref[:, ZC_GATE:ZC_GATE + LANES])

    zc = z_ref[:, ZC_MLA:ZC_MLA + MLA_ROW]
    lane384 = lax.broadcasted_iota(I32, zc.shape, 1)
    sq = jnp.where(lane384 < MLA_KV_LORA, zc * zc, 0.0)
    rstd = lax.rsqrt(jnp.sum(sq, axis=-1, keepdims=True) / MLA_KV_LORA + RMS_EPS)
    normed = zc * rstd * gkv_ref[...]
    x3 = zc[:, 256:384]
    lane128 = lax.broadcasted_iota(I32, x3.shape, 1)
    col3 = jnp.where(lane128 < 64, normed[:, 256:384], _rope_quarter(x3, c64, s64))
    row_ref[:, 0:256] = normed[:, 0:256]
    row_ref[:, 256:384] = col3
    rowb_ref[:, 0:256] = normed[:, 0:256].astype(BF16)
    rowb_ref[:, 256:384] = col3.astype(BF16)

    cq = _rms(z_ref[:, ZC_CQ:ZC_CQ + MLA_Q_LORA], gq_ref[...]).astype(BF16)
    qm = jnp.dot(cq, wuq_ref[...], preferred_element_type=F32)
    pe_base = MLA_HEADS * MLA_NOPE
    for hp in range(MLA_HEADS // 2):
        pe = _rope_quarter(qm[:, pe_base + hp * 128:pe_base + (hp + 1) * 128], c64, s64)
        for sub in range(2):
            h = 2 * hp + sub
            nope = qm[:, h * 128:(h + 1) * 128].astype(BF16)
            ql = jnp.dot(nope, wuk_ref[h], preferred_element_type=F32)
            pe_hi = pe if sub == 1 else pltpu.roll(pe, 64, 1)
            tail = ql[:, 256:384] + jnp.where(lane128 >= 64, pe_hi, 0.0)
            qcat_ref[:, h * 384:h * 384 + 256] = (ql[:, 0:256] * MLA_SCALE).astype(BF16)
            qcat_ref[:, h * 384 + 256:(h + 1) * 384] = (tail * MLA_SCALE).astype(BF16)


def split_projection(z, tabs, tab_index, gq, gkv, wuq, wuk, tm):
    n = z.shape[0]
    c128, s128, c64, s64 = tabs
    row = lambda w: pl.BlockSpec((tm, w), lambda i: (i, 0))
    tab = pl.BlockSpec((tm, LANES), lambda i: (tab_index(i), 0))
    full = lambda a: pl.BlockSpec(a.shape, lambda i: (0,) * a.ndim)
    outs = [((n, NSA_Q_COLS), BF16), ((2 * n, 128), F32), ((2 * n, 128), F32), ((2 * n, 128), F32), ((n, 256), BF16),
            ((n, 256), BF16), ((n, LANES), F32), ((n, MLA_HEADS * MLA_ROW), BF16), ((n, MLA_ROW), F32),
            ((n, MLA_ROW), BF16)]
    return pl.pallas_call(
        _split_kernel,
        out_shape=[jax.ShapeDtypeStruct(s, d) for s, d in outs],
        grid=(n // tm,),
        in_specs=[row(Z_COLS), tab, tab, tab, tab, full(gq), full(gkv), full(wuq), full(wuk)],
        out_specs=[pl.BlockSpec((tm * s[0] // n, s[1]), lambda i: (i, 0)) for s, _ in outs],
        compiler_params=_cparams(("parallel",)),
        name="split_projection",
    )(z, c128, s128, c64, s64, gq, gkv, wuq, wuk)


ROWS_PER_CMP_BLOCK = 2 * CMP_BLOCK


def _compress_blocks(load_rows, pos_ref, w_ref, kv):
    pieces = [(load_rows(2 * l + kv) + pos_ref[2 * l + kv:2 * l + kv + 1, :]).astype(BF16) for l in range(CMP_BLOCK)]
    return jnp.dot(jnp.concatenate(pieces, axis=1), w_ref[kv], preferred_element_type=F32)


def _compress_kernel(x_ref, pos_ref, w_ref, o_ref):
    nblk = o_ref.shape[0]
    for kv in range(2):
        o_ref[:, kv * 128:(kv + 1) * 128] = _compress_blocks(
            lambda r: x_ref[pl.ds(r, nblk, stride=ROWS_PER_CMP_BLOCK), :], pos_ref, w_ref, kv)


def compress_rows(x, pos, w, nblk):
    rows = x.shape[0]
    step = nblk * ROWS_PER_CMP_BLOCK
    return pl.pallas_call(
        _compress_kernel,
        out_shape=jax.ShapeDtypeStruct((rows // ROWS_PER_CMP_BLOCK, 256), F32),
        grid=(rows // step,),
        in_specs=[pl.BlockSpec((step, 128), lambda i: (i, 0)),
                  pl.BlockSpec(pos.shape, lambda i: (0, 0)),
                  pl.BlockSpec(w.shape, lambda i: (0, 0, 0))],
        out_specs=pl.BlockSpec((nblk, 256), lambda i: (i, 0)),
        compiler_params=_cparams(("parallel",)),
        name="compress_prompt",
    )(x, pos, w)


def _nsa_prompt_kernel(q_ref, g_ref, ck_ref, ks_ref, kw_ref, o_ref, *, seq, kc):
    H, QB = NSA_HEADS, QUERY_BLOCK
    n_sel = seq // SEL_BLOCK
    half = ck_ref.shape[1] // 2
    qb = pl.program_id(1)
    start = qb * QB
    Q = jnp.concatenate([q_ref[:, h * 128:(h + 1) * 128] for h in range(H)], axis=0)
    qpos = start + lax.broadcasted_iota(I32, (QB, 1), 0)

    ck = ck_ref[0]
    s_c = lax.dot_general(Q, ck[:, :128].astype(BF16), _CONTRACT_LAST, preferred_element_type=F32)
    lane = lax.broadcasted_iota(I32, (1, 2 * half), 1)
    jj = lane % half
    blk = 2 * jj + (lane >= half).astype(I32)
    mask_c = (((blk + 1) * CMP_BLOCK - 1) <= qpos) & (jj < n_sel)
    p_c = _softmax_heads(s_c.reshape(H, QB, 2 * half), jnp.where(mask_c, 0.0, NEG_INF), mask_c.astype(F32))
    o_c = jnp.dot(p_c.reshape(H * QB, 2 * half).astype(BF16), ck[:, 128:].astype(BF16),
                  preferred_element_type=F32)

    imp = jnp.sum(p_c, axis=0)
    imp = imp[:, :half] + imp[:, half:]
    j = lax.broadcasted_iota(I32, (1, half), 1)
    cur = qpos // SEL_BLOCK
    valid = (j * SEL_BLOCK <= qpos) & (j < n_sel)
    forced = (j == 0) | (j == cur) | (j == cur - 1)
    score = jnp.where(valid, imp + jnp.where(forced, FORCE_BONUS, 0.0), -jnp.inf)
    rank = jnp.zeros((QB, half), F32)
    for i in range(n_sel):
        col = score[:, i:i + 1]
        ahead = (col > score) | ((col == score) & (j > i))
        rank = rank + ahead.astype(F32)
    chosen = ((rank < min(N_SELECT, n_sel)) & valid).astype(BF16)
    jcol = lax.broadcasted_iota(I32, (half, 1), 0)

    def sel_body(c, carry):
        m, l, acc = carry
        k0 = pl.multiple_of(c * kc, kc)
        kk = ks_ref[pl.ds(k0, kc), 0:128]
        vv = ks_ref[pl.ds(k0, kc), 128:256]
        s = lax.dot_general(Q, kk, _CONTRACT_LAST, preferred_element_type=F32).reshape(H, QB, kc)
        kidx = k0 + lax.broadcasted_iota(I32, (1, kc), 1)
        expand = ((kidx // SEL_BLOCK) == jcol).astype(BF16)
        picked = jnp.dot(chosen, expand, preferred_element_type=F32) > 0.5
        ok = picked & (kidx <= qpos)
        s = s + jnp.where(ok, 0.0, NEG_INF)[None]
        m_new = jnp.maximum(m, jnp.max(s, axis=-1, keepdims=True))
        alpha = jnp.exp(m - m_new)
        e = jnp.exp(s - m_new) * ok.astype(F32)[None]
        l = alpha * l + jnp.sum(e, axis=-1, keepdims=True)
        pv = jnp.dot(e.reshape(H * QB, kc).astype(BF16), vv, preferred_element_type=F32)
        acc = alpha.reshape(H * QB, 1) * acc + pv
        return m_new, l, acc

    n_chunks = (start + QB + kc - 1) // kc
    m0 = jnp.full((H, QB, 1), NEG_INF, F32)
    l0 = jnp.zeros((H, QB, 1), F32)
    a0 = jnp.zeros((H * QB, 128), F32)
    _, l_s, acc_s = lax.fori_loop(0, n_chunks, sel_body, (m0, l0, a0))
    o_s = acc_s / jnp.maximum(l_s, TINY).reshape(H * QB, 1)

    wk = WINDOW + QB
    w0 = pl.multiple_of(jnp.clip(start - WINDOW, 0, seq - wk), QB)
    kk = kw_ref[pl.ds(w0, wk), 0:128]
    vv = kw_ref[pl.ds(w0, wk), 128:256]
    s_w = lax.dot_general(Q, kk, _CONTRACT_LAST, preferred_element_type=F32).reshape(H, QB, wk)
    kidx = w0 + lax.broadcasted_iota(I32, (1, wk), 1)
    mask_w = (kidx <= qpos) & (kidx > qpos - WINDOW)
    p_w = _softmax_heads(s_w, jnp.where(mask_w, 0.0, NEG_INF), mask_w.astype(F32))
    o_w = jnp.dot(p_w.reshape(H * QB, wk).astype(BF16), vv, preferred_element_type=F32)

    g = g_ref[...]
    for h in range(H):
        r = slice(h * QB, (h + 1) * QB)
        o_ref[:, h * 128:(h + 1) * 128] = (g[:, h:h + 1] * o_c[r] + g[:, H + h:H + h + 1] * o_s[r]
                                           + g[:, 2 * H + h:2 * H + h + 1] * o_w[r])


def nsa_prompt(q, gates, ck, kvs_b, kvw_b, batch, seq, kc=512):
    nqb = seq // QUERY_BLOCK
    assert seq % kc == 0 and seq >= WINDOW + QUERY_BLOCK
    return pl.pallas_call(
        functools.partial(_nsa_prompt_kernel, seq=seq, kc=kc),
        out_shape=jax.ShapeDtypeStruct((batch * seq, GROUP_WIDTH), F32),
        grid=(batch, nqb),
        in_specs=[pl.BlockSpec((QUERY_BLOCK, GROUP_WIDTH), lambda b, i: (b * nqb + i, 0)),
                  pl.BlockSpec((QUERY_BLOCK, LANES), lambda b, i: (b * nqb + i, 0)),
                  pl.BlockSpec((1,) + ck.shape[1:], lambda b, i: (b, 0, 0)),
                  pl.BlockSpec((seq, 256), lambda b, i: (b, 0)),
                  pl.BlockSpec((seq, 256), lambda b, i: (b, 0))],
        out_specs=pl.BlockSpec((QUERY_BLOCK, GROUP_WIDTH), lambda b, i: (b * nqb + i, 0)),
        compiler_params=_cparams(("parallel", "parallel")),
        name="nsa_prompt",
    )(q, gates, ck, kvs_b, kvw_b)


def _mla_prompt_kernel(q_ref, rows_ref, o_ref, *, kc):
    H, QB = MLA_HEADS, QUERY_BLOCK
    start = pl.program_id(1) * QB
    Q = jnp.concatenate([q_ref[:, h * MLA_ROW:(h + 1) * MLA_ROW] for h in range(H)], axis=0)
    qpos = start + lax.broadcasted_iota(I32, (QB, 1), 0)

    def body(c, carry):
        m, l, acc = carry
        k0 = pl.multiple_of(c * kc, kc)
        rows = rows_ref[pl.ds(k0, kc), :]
        s = lax.dot_general(Q, rows, _CONTRACT_LAST, preferred_element_type=F32).reshape(H, QB, kc)
        ok = (k0 + lax.broadcasted_iota(I32, (1, kc), 1)) <= qpos
        s = s + jnp.where(ok, 0.0, NEG_INF)[None]
        m_new = jnp.maximum(m, jnp.max(s, axis=-1, keepdims=True))
        alpha = jnp.exp(m - m_new)
        e = jnp.exp(s - m_new) * ok.astype(F32)[None]
        l = alpha * l + jnp.sum(e, axis=-1, keepdims=True)
        pv = jnp.dot(e.reshape(H * QB, kc).astype(BF16), rows, preferred_element_type=F32)
        acc = alpha.reshape(H * QB, 1) * acc + pv
        return m_new, l, acc

    n_chunks = (start + QB + kc - 1) // kc
    m0 = jnp.full((H, QB, 1), NEG_INF, F32)
    l0 = jnp.zeros((H, QB, 1), F32)
    a0 = jnp.zeros((H * QB, MLA_ROW), F32)
    _, l, acc = lax.fori_loop(0, n_chunks, body, (m0, l0, a0))
    o = (acc / jnp.maximum(l, TINY).reshape(H * QB, 1)).astype(BF16)
    for h in range(H):
        o_ref[:, h * MLA_ROW:(h + 1) * MLA_ROW] = o[h * QB:(h + 1) * QB]


def mla_prompt(qcat, rows_b, batch, seq, kc=512):
    nqb = seq // QUERY_BLOCK
    width = MLA_HEADS * MLA_ROW
    return pl.pallas_call(
        functools.partial(_mla_prompt_kernel, kc=kc),
        out_shape=jax.ShapeDtypeStruct((batch * seq, width), BF16),
        grid=(batch, nqb),
        in_specs=[pl.BlockSpec((QUERY_BLOCK, width), lambda b, i: (b * nqb + i, 0)),
                  pl.BlockSpec((seq, MLA_ROW), lambda b, i: (b, 0))],
        out_specs=pl.BlockSpec((QUERY_BLOCK, width), lambda b, i: (b * nqb + i, 0)),
        compiler_params=_cparams(("parallel", "parallel")),
        name="mla_prompt",
    )(qcat, rows_b)


def _page_copy(cache_ref, page, buf, slot, p, rows, sem):
    if len(cache_ref.shape) == 2:
        src = cache_ref.at[pl.ds(pl.multiple_of(page * rows, rows), rows), :]
    else:
        src = cache_ref.at[page]
    return pltpu.make_async_copy(src, buf.at[slot, pl.ds(p * rows, rows), :], sem.at[slot])


def _gather_pipeline(pt_ref, cache_ref, buf, sem, ch, rows):
    b, c = pl.program_id(0), pl.program_id(1)
    nc = pl.num_programs(1)
    t = b * nc + c
    total = pl.num_programs(0) * nc
    slot = t % 2

    def start(tb, tc, sl):
        for p in range(ch):
            _page_copy(cache_ref, pt_ref[tb, tc * ch + p], buf, sl, p, rows, sem).start()

    @pl.when(t == 0)
    def _():
        start(0, 0, 0)

    @pl.when(t + 1 < total)
    def _():
        start((t + 1) // nc, (t + 1) % nc, 1 - slot)

    for p in range(ch):
        _page_copy(cache_ref, 0, buf, slot, p, rows, sem).wait()
    return slot


def _cmp_sample_kernel(pt_ref, q_ref, w_ref, pos_ref, cache_ref, oc_ref, imp_ref,
                       buf, sem, ck_scr, *, ch, past, dec_seq):
    c = pl.program_id(1)
    rows_per_page = 2 * PAGE_SIZE
    slot = _gather_pipeline(pt_ref, cache_ref, buf, sem, ch, rows_per_page)
    nsb = ch * rows_per_page // (2 * ROWS_PER_CMP_BLOCK)

    def load_rows(r):
        return jnp.concatenate(
            [buf[slot, pl.ds(parity * ROWS_PER_CMP_BLOCK + r, nsb, stride=2 * ROWS_PER_CMP_BLOCK), :]
             for parity in range(2)], axis=0)

    r0 = pl.multiple_of(c * nsb, nsb)
    for kv in range(2):
        ckp = _compress_blocks(load_rows, pos_ref, w_ref, kv)
        ck_scr[0, kv, pl.ds(r0, nsb), :] = ckp[:nsb]
        ck_scr[1, kv, pl.ds(r0, nsb), :] = ckp[nsb:]

    @pl.when(c == pl.num_programs(1) - 1)
    def _():
        H = NSA_HEADS
        nsel = ck_scr.shape[2]
        ckk = jnp.concatenate([ck_scr[0, 0], ck_scr[1, 0]], axis=0).astype(BF16)
        ckv = jnp.concatenate([ck_scr[0, 1], ck_scr[1, 1]], axis=0).astype(BF16)
        q = q_ref[0]
        s = lax.dot_general(q, ckk, _CONTRACT_LAST, preferred_element_type=F32)
        lane = lax.broadcasted_iota(I32, (1, 2 * nsel), 1)
        blk = 2 * (lane % nsel) + (lane >= nsel).astype(I32)
        qpos = past + lax.broadcasted_iota(I32, (dec_seq * H, 1), 0) // H
        mask = ((blk + 1) * CMP_BLOCK - 1) <= qpos
        s = jnp.where(mask, s, NEG_INF)
        m = jnp.max(s, axis=-1, keepdims=True)
        e = jnp.where(mask, jnp.exp(s - m), 0.0)
        p = e / jnp.maximum(jnp.sum(e, axis=-1, keepdims=True), TINY)
        oc_ref[0] = jnp.dot(p.astype(BF16), ckv, preferred_element_type=F32)
        imp = jnp.sum(p.reshape(dec_seq, H, 2 * nsel), axis=1)
        imp_ref[0] = imp[:, :nsel] + imp[:, nsel:]


def cmp_sample(page_table, q_s, w_cmp, pos_cmp, cache_rows, dec_seq, ch):
    db, n_pages = page_table.shape
    nsel = 2 * n_pages
    rows = dec_seq * NSA_HEADS
    grid_spec = pltpu.PrefetchScalarGridSpec(
        num_scalar_prefetch=1,
        grid=(db, n_pages // ch),
        in_specs=[pl.BlockSpec((1, rows, 128), lambda b, c, pt: (b, 0, 0)),
                  pl.BlockSpec(w_cmp.shape, lambda b, c, pt: (0, 0, 0)),
                  pl.BlockSpec(pos_cmp.shape, lambda b, c, pt: (0, 0)),
                  pl.BlockSpec(memory_space=pl.ANY)],
        out_specs=[pl.BlockSpec((1, rows, 128), lambda b, c, pt: (b, 0, 0)),
                   pl.BlockSpec((1, dec_seq, nsel), lambda b, c, pt: (b, 0, 0))],
        scratch_shapes=[pltpu.VMEM((2, ch * 2 * PAGE_SIZE, 128), F32),
                        pltpu.SemaphoreType.DMA((2,)),
                        pltpu.VMEM((2, 2, nsel, 128), F32)])
    return pl.pallas_call(
        functools.partial(_cmp_sample_kernel, ch=ch, past=n_pages * PAGE_SIZE, dec_seq=dec_seq),
        out_shape=[jax.ShapeDtypeStruct((db, rows, 128), F32), jax.ShapeDtypeStruct((db, dec_seq, nsel), F32)],
        grid_spec=grid_spec,
        compiler_params=_cparams(("arbitrary", "arbitrary")),
        name="cmp_sample",
    )(page_table, q_s, w_cmp, pos_cmp, cache_rows)


def _select_kernel(imp_ref, idx_ref, val_ref, *, past, dec_seq, n_sel):
    nq, n_past = imp_ref.shape
    x = jnp.concatenate([imp_ref[...], jnp.zeros((nq, LANES), F32)], axis=1)
    width = n_past + LANES
    j = lax.broadcasted_iota(I32, (1, width), 1)
    jf = j.astype(F32)
    qpos = past + lax.broadcasted_iota(I32, (nq, 1), 0) % dec_seq
    cur = qpos // SEL_BLOCK
    valid = (j * SEL_BLOCK <= qpos) & (j < n_sel)
    forced = (j == 0) | (j == cur) | (j == cur - 1)
    score = jnp.where(valid, x + jnp.where(forced, FORCE_BONUS, 0.0), -jnp.inf)
    out_lane = lax.broadcasted_iota(I32, (1, LANES), 1)
    idx = jnp.zeros((nq, LANES), F32)
    val = jnp.zeros((nq, LANES), F32)
    for r in range(min(N_SELECT, n_sel)):
        m = jnp.max(score, axis=-1, keepdims=True)
        ix = jnp.min(jnp.where(score == m, jf, float(width)), axis=-1, keepdims=True)
        idx = jnp.where(out_lane == r, ix, idx)
        val = jnp.where(out_lane == r, (m > -jnp.inf).astype(F32), val)
        score = jnp.where(jf == ix, -jnp.inf, score)
    idx_ref[...] = idx.astype(I32)
    val_ref[...] = val.astype(I32)


def select_sample(imp, past, dec_seq, n_sel):
    nq = imp.shape[0]
    return pl.pallas_call(
        functools.partial(_select_kernel, past=past, dec_seq=dec_seq, n_sel=n_sel),
        out_shape=[jax.ShapeDtypeStruct((nq, LANES), I32), jax.ShapeDtypeStruct((nq, LANES), I32)],
        compiler_params=_cparams(None),
        name="select_sample",
    )(imp)


ROWS_PER_SEL_BLOCK = 2 * SEL_BLOCK


def _sel_copy(cache_ref, new_ref, pt_ref, idx_ref, b, e, buf, slot, sem, n_pages):
    n_past_blk = 2 * n_pages
    ix = idx_ref[b, e]
    dst = buf.at[slot, pl.ds(e * ROWS_PER_SEL_BLOCK, ROWS_PER_SEL_BLOCK), :]
    is_new = ix >= n_past_blk
    page = pt_ref[b, jnp.minimum(ix // 2, n_pages - 1)]
    row0 = pl.multiple_of((2 * page + ix % 2) * ROWS_PER_SEL_BLOCK, ROWS_PER_SEL_BLOCK)

    def new_copy():
        return pltpu.make_async_copy(new_ref.at[b], dst, sem.at[slot])

    def past_copy():
        return pltpu.make_async_copy(cache_ref.at[pl.ds(row0, ROWS_PER_SEL_BLOCK), :], dst, sem.at[slot])

    return is_new, new_copy, past_copy


def _masked_softmax(s, ok):
    s = jnp.where(ok, s, NEG_INF)
    m = jnp.max(s, axis=-1, keepdims=True)
    e = jnp.where(ok, jnp.exp(s - m), 0.0)
    return e / jnp.maximum(jnp.sum(e, axis=-1, keepdims=True), TINY)


def _sel_sample_kernel(pt_ref, idx_ref, q_ref, ok_ref, oc_ref, g_ref, win_ref, wnew_ref, wmask_ref,
                       cache_ref, new_ref, o_ref, buf, sem, kw_scr, *, dec_seq, n_pages):
    H = NSA_HEADS
    nsl = dec_seq * N_SELECT
    b = pl.program_id(0)
    nb = pl.num_programs(0)
    slot = b % 2

    def start(tb, sl):
        for e in range(nsl):
            is_new, new_copy, past_copy = _sel_copy(cache_ref, new_ref, pt_ref, idx_ref, tb, e, buf, sl, sem, n_pages)

            @pl.when(is_new)
            def _():
                new_copy().start()

            @pl.when(jnp.logical_not(is_new))
            def _():
                past_copy().start()

    @pl.when(b == 0)
    def _():
        start(0, 0)

    @pl.when(b + 1 < nb)
    def _():
        start(b + 1, 1 - slot)

    for e in range(nsl):
        pltpu.make_async_copy(new_ref.at[0], buf.at[slot, pl.ds(e * ROWS_PER_SEL_BLOCK, ROWS_PER_SEL_BLOCK), :],
                              sem.at[slot]).wait()

    q = q_ref[0]
    nk = N_SELECT * SEL_BLOCK
    keys = buf[slot, pl.ds(0, nsl * SEL_BLOCK, stride=2), :].astype(BF16).reshape(dec_seq, nk, 128)
    vals = buf[slot, pl.ds(1, nsl * SEL_BLOCK, stride=2), :].astype(BF16).reshape(dec_seq, nk, 128)
    q3 = q.reshape(dec_seq, H, 128)
    s = jnp.einsum('shd,skd->shk', q3, keys, preferred_element_type=F32)
    p = _masked_softmax(s, ok_ref[0][:, None, :] > 0.5)
    o_s = jnp.einsum('shk,skd->shd', p.astype(BF16), vals, preferred_element_type=F32).reshape(dec_seq * H, 128)

    wb = win_ref.shape[1] // 2
    nw = wnew_ref.shape[1] // 2
    for kv in range(2):
        kw_scr[kv, 0:wb, :] = win_ref[0, pl.ds(kv, wb, stride=2), :]
        kw_scr[kv, wb:, :] = jnp.zeros((kw_scr.shape[1] - wb, 128), F32)
        kw_scr[kv, wb:wb + nw, :] = wnew_ref[0, pl.ds(kv, nw, stride=2), :]
    s = lax.dot_general(q, kw_scr[0].astype(BF16), _CONTRACT_LAST, preferred_element_type=F32)
    p = _masked_softmax(s, wmask_ref[...] > 0.5)
    o_w = jnp.dot(p.astype(BF16), kw_scr[1].astype(BF16), preferred_element_type=F32)

    g = g_ref[0]
    o_ref[0] = g[:, 0:1] * oc_ref[0] + g[:, 1:2] * o_s + g[:, 2:3] * o_w


def sel_sample(page_table, idx, q_s, okmask, o_c, gates_rows, win, wnew, wmask, cache_sel, new_blocks, dec_seq):
    db, n_pages = page_table.shape
    rows = dec_seq * NSA_HEADS
    nkeys = N_SELECT * SEL_BLOCK
    wk = wmask.shape[1]
    b3 = lambda shape: pl.BlockSpec((1,) + shape, lambda b, pt, ix: (b, 0, 0))
    grid_spec = pltpu.PrefetchScalarGridSpec(
        num_scalar_prefetch=2,
        grid=(db,),
        in_specs=[b3((rows, 128)), b3((dec_seq, nkeys)), b3((rows, 128)), b3((rows, 3)),
                  b3(win.shape[1:]), b3(wnew.shape[1:]),
                  pl.BlockSpec(wmask.shape, lambda b, pt, ix: (0, 0)),
                  pl.BlockSpec(memory_space=pl.ANY), pl.BlockSpec(memory_space=pl.ANY)],
        out_specs=b3((rows, 128)),
        scratch_shapes=[pltpu.VMEM((2, dec_seq * N_SELECT * ROWS_PER_SEL_BLOCK, 128), F32),
                        pltpu.SemaphoreType.DMA((2,)),
                        pltpu.VMEM((2, wk, 128), F32)])
    return pl.pallas_call(
        functools.partial(_sel_sample_kernel, dec_seq=dec_seq, n_pages=n_pages),
        out_shape=jax.ShapeDtypeStruct((db, rows, 128), F32),
        grid_spec=grid_spec,
        compiler_params=_cparams(("arbitrary",)),
        name="sel_win_sample",
    )(page_table, idx, q_s, okmask, o_c, gates_rows, win, wnew, wmask, cache_sel, new_blocks)


def _mla_sample_kernel(pt_ref, q_ref, new_ref, cache_ref, o_ref, buf, sem, m_scr, l_scr, acc_scr, new_scr,
                       *, ch, dec_seq):
    H = MLA_HEADS
    c = pl.program_id(1)
    slot = _gather_pipeline(pt_ref, cache_ref, buf, sem, ch, PAGE_SIZE)
    q = q_ref[0]

    @pl.when(c == 0)
    def _():
        m_scr[...] = jnp.full(m_scr.shape, NEG_INF, F32)
        l_scr[...] = jnp.zeros(l_scr.shape, F32)
        acc_scr[...] = jnp.zeros(acc_scr.shape, F32)

    def update(rows, ok):
        s = lax.dot_general(q, rows, _CONTRACT_LAST, preferred_element_type=F32)
        if ok is not None:
            s = jnp.where(ok, s, NEG_INF)
        m_old = m_scr[...]
        m_new = jnp.maximum(m_old, jnp.max(s, axis=-1, keepdims=True))
        alpha = jnp.exp(m_old - m_new)
        e = jnp.exp(s - m_new)
        if ok is not None:
            e = jnp.where(ok, e, 0.0)
        l_scr[...] = alpha * l_scr[...] + jnp.sum(e, axis=-1, keepdims=True)
        acc_scr[...] = alpha * acc_scr[...] + jnp.dot(e.astype(BF16), rows, preferred_element_type=F32)
        m_scr[...] = m_new

    update(buf[slot].astype(BF16), None)

    @pl.when(c == pl.num_programs(1) - 1)
    def _():
        nr = new_ref.shape[1]
        new_scr[...] = jnp.zeros(new_scr.shape, BF16)
        new_scr[0:nr, :] = new_ref[0]
        t = lax.broadcasted_iota(I32, (1, new_scr.shape[0]), 1)
        srow = lax.broadcasted_iota(I32, (dec_seq * H, 1), 0) // H
        update(new_scr[...], (t <= srow) & (t < dec_seq))
        o_ref[0] = (acc_scr[...] / jnp.maximum(l_scr[...], TINY)).astype(BF16)


def mla_sample(page_table, qcat_s, new_rows, cache, dec_seq, ch):
    db, n_pages = page_table.shape
    rows = dec_seq * MLA_HEADS
    grid_spec = pltpu.PrefetchScalarGridSpec(
        num_scalar_prefetch=1,
        grid=(db, n_pages // ch),
        in_specs=[pl.BlockSpec((1, rows, MLA_ROW), lambda b, c, pt: (b, 0, 0)),
                  pl.BlockSpec((1,) + new_rows.shape[1:], lambda b, c, pt: (b, 0, 0)),
                  pl.BlockSpec(memory_space=pl.ANY)],
        out_specs=pl.BlockSpec((1, rows, MLA_ROW), lambda b, c, pt: (b, 0, 0)),
        scratch_shapes=[pltpu.VMEM((2, ch * PAGE_SIZE, MLA_ROW), F32),
                        pltpu.SemaphoreType.DMA((2,)),
                        pltpu.VMEM((rows, 1), F32), pltpu.VMEM((rows, 1), F32),
                        pltpu.VMEM((rows, MLA_ROW), F32), pltpu.VMEM((LANES, MLA_ROW), BF16)])
    return pl.pallas_call(
        functools.partial(_mla_sample_kernel, ch=ch, dec_seq=dec_seq),
        out_shape=jax.ShapeDtypeStruct((db, rows, MLA_ROW), BF16),
        grid_spec=grid_spec,
        compiler_params=_cparams(("arbitrary", "arbitrary")),
        name="mla_sample",
    )(page_table, qcat_s, new_rows, cache)


def _head_proj_kernel(x_ref, w_ref, o_ref):
    for h in range(MLA_HEADS):
        o_ref[:, h * MLA_V:(h + 1) * MLA_V] = jnp.dot(x_ref[:, h * MLA_ROW:(h + 1) * MLA_ROW], w_ref[h],
                                                      preferred_element_type=F32)


def head_proj(x, w, tm):
    n = x.shape[0]
    return pl.pallas_call(
        _head_proj_kernel,
        out_shape=jax.ShapeDtypeStruct((n, GROUP_WIDTH), F32),
        grid=(n // tm,),
        in_specs=[pl.BlockSpec((tm, x.shape[1]), lambda i: (i, 0)),
                  pl.BlockSpec(w.shape, lambda i: (0, 0, 0))],
        out_specs=pl.BlockSpec((tm, GROUP_WIDTH), lambda i: (i, 0)),
        compiler_params=_cparams(("parallel",)),
        name="mla_value_proj",
    )(x, w)


def _out_proj_kernel(a_ref, b_ref, ga_ref, gb_ref, w_ref, h_ref, o_ref, cat_ref):
    @pl.when(pl.program_id(1) == 0)
    def _():
        cat_ref[:, 0:GROUP_WIDTH] = _rms(a_ref[...], ga_ref[...]).astype(BF16)
        cat_ref[:, GROUP_WIDTH:2 * GROUP_WIDTH] = _rms(b_ref[...], gb_ref[...]).astype(BF16)

    o_ref[...] = h_ref[...] + jnp.dot(cat_ref[...], w_ref[...], preferred_element_type=F32)


def out_proj(o_nsa, o_mla, ga, gb, w, h, tm, tn):
    n, d = h.shape
    gw = GROUP_WIDTH
    return pl.pallas_call(
        _out_proj_kernel,
        out_shape=jax.ShapeDtypeStruct((n, d), F32),
        grid=(n // tm, d // tn),
        in_specs=[pl.BlockSpec((tm, gw), lambda i, j: (i, 0)),
                  pl.BlockSpec((tm, gw), lambda i, j: (i, 0)),
                  pl.BlockSpec((1, gw), lambda i, j: (0, 0)),
                  pl.BlockSpec((1, gw), lambda i, j: (0, 0)),
                  pl.BlockSpec((2 * gw, tn), lambda i, j: (0, j)),
                  pl.BlockSpec((tm, tn), lambda i, j: (i, j))],
        out_specs=pl.BlockSpec((tm, tn), lambda i, j: (i, j)),
        scratch_shapes=[pltpu.VMEM((tm, 2 * gw), BF16)],
        compiler_params=_cparams(("parallel", "arbitrary")),
        name="out_proj",
    )(o_nsa, o_mla, ga, gb, w, h)


def _router_kernel(h_ref, g_ref, w_ref, b_ref, xn_ref, ids_ref, gts_ref):
    xn = _rms(h_ref[...], g_ref[...])
    xn_ref[...] = xn
    logits = jnp.dot(xn, w_ref[...], preferred_element_type=F32, precision=lax.Precision.HIGHEST) + b_ref[...]
    lane = lax.broadcasted_iota(I32, logits.shape, 1)
    lanef = lane.astype(F32)
    big = float(LANES)

    def first_argmax(v):
        m = jnp.max(v, axis=-1, keepdims=True)
        return m, jnp.min(jnp.where(v == m, lanef, big), axis=-1, keepdims=True)

    gl = jnp.where(lane < N_GROUPS, logits, -jnp.inf)
    gmax, gidx = first_argmax(gl)
    g_gate = 1.0 / jnp.sum(jnp.exp(gl - gmax), axis=-1, keepdims=True)
    group_of_lane = ((lane - N_GROUPS) // EXPERTS_PER_GROUP).astype(F32)
    in_group = (lane >= N_GROUPS) & (lane < N_GROUPS + N_EXPERTS) & (group_of_lane == gidx)
    el = jnp.where(in_group, logits, -jnp.inf)
    v0, i0 = first_argmax(el)
    v1, i1 = first_argmax(jnp.where(lanef == i0, -jnp.inf, el))
    e0 = jnp.exp(v0 - v0)
    e1 = jnp.exp(v1 - v0)
    den = e0 + e1
    ids = jnp.where(lane == 0, i0 - N_GROUPS, jnp.where(lane == 1, i1 - N_GROUPS, 0.0))
    gts = jnp.where(lane == 0, g_gate * (e0 / den), jnp.where(lane == 1, g_gate * (e1 / den), 0.0))
    ids_ref[...] = ids.astype(I32)
    gts_ref[...] = gts


def router(h, g, w, b, tm):
    n, d = h.shape
    return pl.pallas_call(
        _router_kernel,
        out_shape=[jax.ShapeDtypeStruct((n, d), F32), jax.ShapeDtypeStruct((n, LANES), I32),
                   jax.ShapeDtypeStruct((n, LANES), F32)],
        grid=(n // tm,),
        in_specs=[pl.BlockSpec((tm, d), lambda i: (i, 0)),
                  pl.BlockSpec((1, d), lambda i: (0, 0)),
                  pl.BlockSpec((d, LANES), lambda i: (0, 0)),
                  pl.BlockSpec((1, LANES), lambda i: (0, 0))],
        out_specs=[pl.BlockSpec((tm, d), lambda i: (i, 0)),
                   pl.BlockSpec((tm, LANES), lambda i: (i, 0)),
                   pl.BlockSpec((tm, LANES), lambda i: (i, 0))],
        compiler_params=_cparams(("parallel",)),
        name="ffn_norm_router",
    )(h, g.reshape(1, d), w, b)


ROW_DMA_WAIT_GROUP = 8


def _ffn_kernel(be_ref, rs_ref, cnt_ref, nu_ref, order_ref, x_hbm, wg_ref, wu_ref, wd_ref, y_hbm,
                xf_ref, xb_ref, acc_ref, gsem, ssem, *, n_tok):
    i, f = pl.program_id(0), pl.program_id(1)
    nf = pl.num_programs(1)
    nu = nu_ref[0]
    grp = ROW_DMA_WAIT_GROUP

    def gather_row(blk, r):
        tok = order_ref[rs_ref[blk] + r] // 2
        return pltpu.make_async_copy(x_hbm.at[pl.ds(tok, 1), :], xf_ref.at[pl.ds(r, 1), :], gsem)

    def scatter_row(blk, r):
        entry = order_ref[rs_ref[blk] + r]
        dst = (entry % 2) * n_tok + entry // 2
        return pltpu.make_async_copy(acc_ref.at[pl.ds(r, 1), :], y_hbm.at[pl.ds(dst, 1), :], ssem)

    def start_rows(blk, row_copy):
        def body(r, carry):
            row_copy(blk, r).start()
            return carry
        lax.fori_loop(0, cnt_ref[blk], body, 0)

    def wait_rows(blk, row_copy, group_copy):
        n = cnt_ref[blk]

        def groups(g, carry):
            group_copy().wait()
            return carry

        def singles(r, carry):
            row_copy(blk, 0).wait()
            return carry
        lax.fori_loop(0, n // grp, groups, 0)
        lax.fori_loop(0, n % grp, singles, 0)

    def gather_group():
        return pltpu.make_async_copy(x_hbm.at[pl.ds(0, grp), :], xf_ref.at[pl.ds(0, grp), :], gsem)

    def scatter_group():
        return pltpu.make_async_copy(acc_ref.at[pl.ds(0, grp), :], y_hbm.at[pl.ds(0, grp), :], ssem)

    @pl.when((i < nu) & (f == 0))
    def _():
        @pl.when(i == 0)
        def _():
            xf_ref[...] = jnp.zeros(xf_ref.shape, F32)
            start_rows(0, gather_row)

        wait_rows(i, gather_row, gather_group)
        xb_ref[...] = xf_ref[...].astype(BF16)

        @pl.when(i + 1 < nu)
        def _():
            start_rows(i + 1, gather_row)

        @pl.when(i > 0)
        def _():
            wait_rows(i - 1, scatter_row, scatter_group)

    @pl.when(i < nu)
    def _():
        x = xb_ref[...]
        gate = jnp.dot(x, wg_ref[...].astype(BF16), preferred_element_type=F32)
        up = jnp.dot(x, wu_ref[...].astype(BF16), preferred_element_type=F32)
        hmid = (jax.nn.silu(gate) * up).astype(BF16)
        y = jnp.dot(hmid, wd_ref[...].astype(BF16), preferred_element_type=F32)

        @pl.when(f == 0)
        def _():
            acc_ref[...] = y

        @pl.when(f > 0)
        def _():
            acc_ref[...] += y

        @pl.when(f == nf - 1)
        def _():
            start_rows(i, scatter_row)

            @pl.when(i == nu - 1)
            def _():
                wait_rows(i, scatter_row, scatter_group)


def expert_ffn(block_expert, row_start, row_count, n_used, order, xn, w_gate, w_up, w_down, tm, tf):
    n_tok, d = xn.shape
    d_exp = w_gate.shape[2]
    nf = d_exp // tf
    n_blocks = block_expert.shape[0]

    def blk(i, nu):
        return jnp.minimum(i, nu[0] - 1)

    def fidx(i, f, nu):
        return jnp.where(i < nu[0], f, nf - 1)

    grid_spec = pltpu.PrefetchScalarGridSpec(
        num_scalar_prefetch=5,
        grid=(n_blocks, nf),
        in_specs=[pl.BlockSpec(memory_space=pl.ANY),
                  pl.BlockSpec((None, d, tf), lambda i, f, be, rs, cn, nu, od: (be[blk(i, nu)], 0, fidx(i, f, nu))),
                  pl.BlockSpec((None, d, tf), lambda i, f, be, rs, cn, nu, od: (be[blk(i, nu)], 0, fidx(i, f, nu))),
                  pl.BlockSpec((None, tf, d), lambda i, f, be, rs, cn, nu, od: (be[blk(i, nu)], fidx(i, f, nu), 0))],
        out_specs=pl.BlockSpec(memory_space=pl.ANY),
        scratch_shapes=[pltpu.VMEM((tm, d), F32), pltpu.VMEM((tm, d), BF16), pltpu.VMEM((tm, d), F32),
                        pltpu.SemaphoreType.DMA(()), pltpu.SemaphoreType.DMA(())])
    return pl.pallas_call(
        functools.partial(_ffn_kernel, n_tok=n_tok),
        out_shape=jax.ShapeDtypeStruct((TOPK_IN_GROUP * n_tok, d), F32),
        grid_spec=grid_spec,
        compiler_params=_cparams(("arbitrary", "arbitrary")),
        name="expert_ffn",
    )(block_expert, row_start, row_count, n_used, order, xn, w_gate, w_up, w_down)


def _combine_kernel(h_ref, y0_ref, y1_ref, g_ref, o_ref):
    g = g_ref[...]
    o_ref[...] = h_ref[...] + (g[:, 0:1] * y0_ref[...] + g[:, 1:2] * y1_ref[...])


def moe_combine(h, y, gts, tm):
    n, d = h.shape
    nt = n // tm
    return pl.pallas_call(
        _combine_kernel,
        out_shape=jax.ShapeDtypeStruct((n, d), F32),
        grid=(nt,),
        in_specs=[pl.BlockSpec((tm, d), lambda i: (i, 0)),
                  pl.BlockSpec((tm, d), lambda i: (i, 0)),
                  pl.BlockSpec((tm, d), lambda i: (i + nt, 0)),
                  pl.BlockSpec((tm, LANES), lambda i: (i, 0))],
        out_specs=pl.BlockSpec((tm, d), lambda i: (i, 0)),
        compiler_params=_cparams(("parallel",)),
        name="moe_combine",
    )(h, y, y, gts)


def _ple_kernel(h_ref, hc_ref, g_ref, wg_ref, p_ref, wp_ref, o_ref, hn_ref):
    @pl.when(pl.program_id(1) == 0)
    def _():
        hn_ref[...] = _rms(h_ref[...], g_ref[...]).astype(BF16)

    gate = jax.nn.sigmoid(jnp.dot(hn_ref[...], wg_ref[...], preferred_element_type=F32))
    proj = jnp.dot(p_ref[...], wp_ref[...], preferred_element_type=F32)
    o_ref[...] = hc_ref[...] + gate * proj


def ple_add(h, g, wg, p, wp, tm, tn):
    n, d = h.shape
    pd = p.shape[1]
    return pl.pallas_call(
        _ple_kernel,
        out_shape=jax.ShapeDtypeStruct((n, d), F32),
        grid=(n // tm, d // tn),
        in_specs=[pl.BlockSpec((tm, d), lambda i, j: (i, 0)),
                  pl.BlockSpec((tm, tn), lambda i, j: (i, j)),
                  pl.BlockSpec((1, d), lambda i, j: (0, 0)),
                  pl.BlockSpec((d, tn), lambda i, j: (0, j)),
                  pl.BlockSpec((tm, pd), lambda i, j: (i, 0)),
                  pl.BlockSpec((pd, tn), lambda i, j: (0, j))],
        out_specs=pl.BlockSpec((tm, tn), lambda i, j: (i, j)),
        scratch_shapes=[pltpu.VMEM((tm, d), BF16)],
        compiler_params=_cparams(("parallel", "arbitrary")),
        name="ple_add",
    )(h, h, g.reshape(1, d), wg, p, wp)


def _final_norm_kernel(x_ref, g_ref, o_ref):
    o_ref[...] = _rms(x_ref[...], g_ref[...])


def final_norm(x, g, tm):
    n, d = x.shape
    return pl.pallas_call(
        _final_norm_kernel,
        out_shape=jax.ShapeDtypeStruct((n, d), F32),
        grid=(n // tm,),
        in_specs=[pl.BlockSpec((tm, d), lambda i: (i, 0)), pl.BlockSpec((1, d), lambda i: (0, 0))],
        out_specs=pl.BlockSpec((tm, d), lambda i: (i, 0)),
        compiler_params=_cparams(("parallel",)),
        name="final_norm",
    )(x, g.reshape(1, d))


def _rope_tables(pos, tm):
    pos = np.asarray(pos, np.float32)[:, None]

    def table(dim):
        inv = np.float32(ROPE_THETA) ** (-np.arange(0, dim, 2, dtype=np.float32) / np.float32(dim))
        ang = (pos * inv[None, :]).astype(np.float32).astype(np.float64)
        cos, sin = np.cos(ang).astype(np.float32), np.sin(ang).astype(np.float32)
        reps = LANES // dim
        return np.tile(np.concatenate([cos, cos], 1), (1, reps)), np.tile(np.concatenate([-sin, sin], 1), (1, reps))

    c128, s128 = table(HEAD_DIM)
    c64, s64 = table(MLA_ROPE)
    return tuple(jnp.asarray(t) for t in (c128, s128, c64, s64))


def _compress_weights(cmp_pos, cmp_w):
    w = cmp_w.reshape(2, CMP_BLOCK * HEAD_DIM, HEAD_DIM).astype(BF16)
    pos = jnp.transpose(cmp_pos, (1, 0, 2)).reshape(ROWS_PER_CMP_BLOCK, HEAD_DIM)
    return w, pos


def _layer(hp, hs, p_all, cache_cmp, cache_sel, cache_mla, win_state, page_table, wts):
    (norm_attn, w_in, cmp_pos, cmp_w, mla_q_norm, mla_kv_norm, mla_w_uq, mla_w_uk, mla_w_uv, group_norm, w_out,
     norm_ffn, rg_w, rg_b, re_w, re_b, w_gate, w_up, w_down, ple_proj, ple_norm, ple_gate_w) = wts
    batch, seq, d_model = hp.shape
    db, dec_seq, _ = hs.shape
    n_pages = page_table.shape[1]
    past = n_pages * PAGE_SIZE
    n_p, n_s = batch * seq, db * dec_seq
    n = n_p + n_s
    tm = int(np.gcd(np.gcd(512, n_s), seq))
    assert n_p % tm == 0 and n_s % tm == 0 and seq % tm == 0 and tm % dec_seq == 0
    assert dec_seq < CMP_BLOCK and dec_seq <= SEL_BLOCK and past % SEL_BLOCK == 0 and past >= WINDOW
    assert win_state.shape[1] == WINDOW

    h = jnp.concatenate([hp.reshape(n_p, d_model), hs.reshape(n_s, d_model)], axis=0)

    c0, c1 = NSA_Q_COLS, NSA_Q_COLS + NSA_KV_COLS
    c2 = c1 + NSA_GATE_COLS
    w_in_p = jnp.concatenate([w_in[:, :c1], w_in[:, c2:], w_in[:, c1:c2],
                              jnp.zeros((d_model, LANES - NSA_GATE_COLS), w_in.dtype)], axis=1).astype(BF16)
    z = norm_matmul(h, norm_attn, w_in_p, tm, 384)

    tm2 = min(256, tm)
    pos_rows = np.concatenate([np.arange(seq), past + np.arange(tm2) % dec_seq])
    tabs = _rope_tables(pos_rows, tm2)
    n_ptiles, seq_tiles = n_p // tm2, seq // tm2
    tab_index = lambda i: jnp.where(i < n_ptiles, i % seq_tiles, seq_tiles)
    wuq = jnp.concatenate([mla_w_uq[:, :, :MLA_NOPE].reshape(MLA_Q_LORA, -1),
                           mla_w_uq[:, :, MLA_NOPE:].reshape(MLA_Q_LORA, -1)], axis=1).astype(BF16)
    wuk = jnp.pad(jnp.transpose(mla_w_uk, (1, 2, 0)), ((0, 0), (0, 0), (0, MLA_ROPE))).astype(BF16)
    gkv = jnp.pad(mla_kv_norm, (0, MLA_ROPE)).reshape(1, MLA_ROW)
    (q_nsa, kv_c, kv_s, kv_w, kvs_b, kvw_b, gates, qcat, mla_row, mla_row_b) = split_projection(
        z, tabs, tab_index, mla_q_norm.reshape(1, -1), gkv, wuq, wuk, tm2)

    w_cmp, pos_cmp = _compress_weights(cmp_pos, cmp_w)
    nb = seq // CMP_BLOCK
    ck = compress_rows(kv_c[:2 * n_p], pos_cmp, w_cmp, min(256, n_p // CMP_BLOCK))
    half = -(-(nb // 2) // LANES) * LANES
    ck = ck.reshape(batch, nb // 2, 2, 256)
    padh = ((0, 0), (0, half - nb // 2), (0, 0))
    ck = jnp.concatenate([jnp.pad(ck[:, :, 0], padh), jnp.pad(ck[:, :, 1], padh)], axis=1)
    o_nsa_p = nsa_prompt(q_nsa, gates, ck, kvs_b, kvw_b, batch, seq)

    o_lat_p = mla_prompt(qcat, mla_row_b, batch, seq)

    rows = dec_seq * NSA_HEADS
    q_s = q_nsa[n_p:].reshape(db, rows, HEAD_DIM)
    ch_c = min(64, n_pages)
    o_c, imp = cmp_sample(page_table, q_s, w_cmp, pos_cmp, cache_cmp.reshape(-1, HEAD_DIM), dec_seq, ch_c)
    n_sel = -(-(past + dec_seq) // SEL_BLOCK)
    n_past_blk = past // SEL_BLOCK
    idx_pad, val_pad = select_sample(imp.reshape(n_s, 2 * n_pages), past, dec_seq, n_sel)
    idx = idx_pad[:, :N_SELECT]
    val = val_pad[:, :N_SELECT] > 0
    qpos_s = past + jnp.arange(n_s, dtype=I32) % dec_seq
    kpos = idx[:, :, None] * SEL_BLOCK + jnp.arange(SEL_BLOCK, dtype=I32)
    okmask = (val[:, :, None] & (kpos <= qpos_s[:, None, None])).astype(F32).reshape(db, dec_seq, N_SELECT * SEL_BLOCK)
    kvs_new = kv_s[2 * n_p:].reshape(db, 2 * dec_seq, HEAD_DIM)
    new_blocks = jnp.pad(kvs_new, ((0, 0), (0, ROWS_PER_SEL_BLOCK - 2 * dec_seq), (0, 0)))
    kvw_new = kv_w[2 * n_p:].reshape(db, 2 * dec_seq, HEAD_DIM)
    wnew = jnp.pad(kvw_new, ((0, 0), (0, -(2 * dec_seq) % 16), (0, 0)))
    wk = WINDOW + LANES
    kp = np.concatenate([past - WINDOW + np.arange(WINDOW), past + np.arange(dec_seq),
                         np.full(wk - WINDOW - dec_seq, -1)])
    qp = past + np.arange(rows) // NSA_HEADS
    wmask = ((kp[None, :] <= qp[:, None]) & (kp[None, :] > qp[:, None] - WINDOW) & (kp[None, :] >= 0))
    wmask = jnp.asarray(wmask.astype(np.float32))
    g_s = gates[n_p:, :NSA_GATE_COLS].reshape(db, dec_seq, 3, NSA_HEADS)
    g_rows = jnp.transpose(g_s, (0, 1, 3, 2)).reshape(db, rows, 3)
    win = win_state.reshape(db, 2 * WINDOW, HEAD_DIM)
    o_nsa_s = sel_sample(page_table, idx.reshape(db, dec_seq * N_SELECT), q_s, okmask, o_c, g_rows, win, wnew, wmask,
                         cache_sel.reshape(-1, HEAD_DIM), new_blocks, dec_seq)
    new_win = jnp.concatenate([win[:, 2 * dec_seq:], kvw_new], axis=1)

    qcat_s = qcat[n_p:].reshape(db, dec_seq * MLA_HEADS, MLA_ROW)
    mla_new = jnp.pad(mla_row_b[n_p:].reshape(db, dec_seq, MLA_ROW), ((0, 0), (0, 16 - dec_seq), (0, 0)))
    o_lat_s = mla_sample(page_table, qcat_s, mla_new, cache_mla, dec_seq, min(32, n_pages))

    o_nsa = jnp.concatenate([o_nsa_p, o_nsa_s.reshape(n_s, GROUP_WIDTH)], axis=0)
    o_lat = jnp.concatenate([o_lat_p, o_lat_s.reshape(n_s, MLA_HEADS * MLA_ROW)], axis=0)
    wuv = jnp.pad(jnp.transpose(mla_w_uv, (1, 0, 2)), ((0, 0), (0, MLA_ROPE), (0, 0))).astype(BF16)
    o_mla = head_proj(o_lat, wuv, tm)
    h1 = out_proj(o_nsa, o_mla, group_norm[0:1], group_norm[1:2], w_out.astype(BF16), h, tm, 512)

    w_r = jnp.concatenate([rg_w, re_w, jnp.zeros((d_model, LANES - N_GROUPS - N_EXPERTS), F32)], axis=1)
    b_r = jnp.concatenate([rg_b, re_b, jnp.zeros((LANES - N_GROUPS - N_EXPERTS,), F32)]).reshape(1, LANES)
    xn, ids, gts = router(h1, norm_ffn, w_r, b_r, tm2)
    tmf = 384
    flat_e = ids[:, :TOPK_IN_GROUP].reshape(-1)
    nk = flat_e.shape[0]
    onehot = (flat_e[:, None] == jnp.arange(N_EXPERTS, dtype=I32)[None, :]).astype(I32)
    within = jnp.sum((jnp.cumsum(onehot, axis=0) - onehot) * onehot, axis=1)
    counts = jnp.sum(onehot, axis=0)
    starts = jnp.cumsum(counts) - counts
    order = jnp.zeros((nk,), I32).at[starts[flat_e] + within].set(jnp.arange(nk, dtype=I32))
    blocks_per_expert = (counts + tmf - 1) // tmf
    blk_end = jnp.cumsum(blocks_per_expert)
    n_used = blk_end[-1].astype(I32).reshape(1)
    n_blocks = -(-(nk + N_EXPERTS * (tmf - 1)) // tmf)
    bi = jnp.arange(n_blocks, dtype=I32)
    block_expert = jnp.minimum(jnp.searchsorted(blk_end, bi, side='right'), N_EXPERTS - 1).astype(I32)
    kb = bi - (blk_end - blocks_per_expert)[block_expert]
    row_start = (starts[block_expert] + kb * tmf).astype(I32)
    row_count = jnp.where(bi < n_used[0], jnp.clip(counts[block_expert] - kb * tmf, 0, tmf), 0).astype(I32)
    row_start = jnp.where(bi < n_used[0], row_start, 0)
    y_rows = expert_ffn(block_expert, row_start, row_count, n_used, order, xn, w_gate, w_up, w_down, tmf, 256)
    h2 = moe_combine(h1, y_rows, gts, tm2)

    h3 = ple_add(h2, ple_norm, ple_gate_w.astype(BF16), p_all.astype(BF16), ple_proj.astype(BF16), tm, 512)

    caches = dict(
        cmp_p=kv_c[:2 * n_p].reshape(batch, seq, 2, HEAD_DIM), cmp_s=kv_c[2 * n_p:].reshape(db, dec_seq, 2, HEAD_DIM),
        sel_p=kv_s[:2 * n_p].reshape(batch, seq, 2, HEAD_DIM), sel_s=kv_s[2 * n_p:].reshape(db, dec_seq, 2, HEAD_DIM),
        mla_p=mla_row[:n_p].reshape(batch, seq, MLA_ROW), mla_s=mla_row[n_p:].reshape(db, dec_seq, MLA_ROW),
        win_p=kv_w[:2 * n_p].reshape(batch, 2 * seq, HEAD_DIM)[:, 2 * (seq - min(WINDOW, seq)):].reshape(
            batch, min(WINDOW, seq), 2, HEAD_DIM),
        win_s=new_win.reshape(db, WINDOW, 2, HEAD_DIM))
    return h3[:n_p].reshape(batch, seq, d_model), h3[n_p:].reshape(db, dec_seq, d_model), caches


def kernel(x_prompt, x_sample, p_prompt, p_sample, cache_nsa_cmp, cache_nsa_sel, cache_mla, state_nsa_win, page_table, norm_attn, w_in, nsa_cmp_pos, nsa_cmp_w, mla_q_norm, mla_kv_norm, mla_w_uq, mla_w_uk, mla_w_uv, group_norm, w_out, norm_ffn, router_group_w, router_group_b, router_expert_w, router_expert_b, w_gate, w_up, w_down, ple_proj, ple_norm, ple_gate_w, norm_final):
    depth = w_in.shape[0]
    hp, hs = x_prompt, x_sample
    per_layer = []
    for i in range(depth):
        ple_dim = p_prompt.shape[-1]
        p_all = jnp.concatenate([p_prompt[i].reshape(-1, ple_dim), p_sample[i].reshape(-1, ple_dim)], axis=0)
        wts = (norm_attn[i], w_in[i], nsa_cmp_pos[i], nsa_cmp_w[i], mla_q_norm[i], mla_kv_norm[i], mla_w_uq[i],
               mla_w_uk[i], mla_w_uv[i], group_norm[i], w_out[i], norm_ffn[i], router_group_w[i], router_group_b[i],
               router_expert_w[i], router_expert_b[i], w_gate[i], w_up[i], w_down[i], ple_proj[i], ple_norm[i],
               ple_gate_w[i])
        hp, hs, caches = _layer(hp, hs, p_all, cache_nsa_cmp[i], cache_nsa_sel[i], cache_mla[i], state_nsa_win[i],
                                page_table, wts)
        per_layer.append(caches)
    d_model = hp.shape[-1]
    n_p = hp.shape[0] * hp.shape[1]
    n_s = hs.shape[0] * hs.shape[1]
    y = final_norm(jnp.concatenate([hp.reshape(n_p, d_model), hs.reshape(n_s, d_model)], axis=0), norm_final,
                   int(np.gcd(np.gcd(512, n_s), n_p)))
    y_prompt = y[:n_p].reshape(hp.shape)
    y_sample = y[n_p:].reshape(hs.shape)
    stack = lambda k: jnp.stack([c[k] for c in per_layer])
    return (y_prompt, y_sample, stack('cmp_p'), stack('cmp_s'), stack('sel_p'), stack('sel_s'),
            stack('mla_p'), stack('mla_s'), stack('win_p'), stack('win_s'))
```

```python
import functools

import numpy as np
import jax
import jax.numpy as jnp
from jax import lax
from jax.experimental import pallas as pl
from jax.experimental.pallas import tpu as pltpu

F32 = jnp.float32
BF16 = jnp.bfloat16
I32 = jnp.int32

PAGE_SIZE = 128
HEAD_DIM = 128
NSA_HEADS = 16
CMP_BLOCK = 32
SEL_BLOCK = 64
N_SELECT = 16
WINDOW = 512
MLA_HEADS = 16
MLA_Q_LORA = 896
MLA_KV_LORA = 320
MLA_NOPE = 128
MLA_ROPE = 64
MLA_V = 128
MLA_ROW = MLA_KV_LORA + MLA_ROPE
N_GROUPS = 8
EXPERTS_PER_GROUP = 8
N_EXPERTS = N_GROUPS * EXPERTS_PER_GROUP
TOPK_IN_GROUP = 2
ROPE_THETA = 10000.0
RMS_EPS = 1e-6
QUERY_BLOCK = 128
FORCE_BONUS = 1e4
NEG_INF = -1e30
TINY = 1e-30
NSA_SCALE = HEAD_DIM ** -0.5
MLA_SCALE = (MLA_NOPE + MLA_ROPE) ** -0.5
NSA_Q_COLS = NSA_HEADS * HEAD_DIM
NSA_KV_COLS = 3 * 2 * HEAD_DIM
NSA_GATE_COLS = 3 * NSA_HEADS
GROUP_WIDTH = NSA_HEADS * HEAD_DIM

LANES = 128
VMEM_LIMIT = 56 * 1024 * 1024

ZC_Q = 0
ZC_KV = ZC_Q + NSA_Q_COLS
ZC_CQ = ZC_KV + NSA_KV_COLS
ZC_MLA = ZC_CQ + MLA_Q_LORA
ZC_GATE = ZC_MLA + MLA_ROW
Z_COLS = ZC_GATE + LANES

_CONTRACT_LAST = (((1,), (1,)), ((), ()))


def _cparams(sem, vmem=VMEM_LIMIT):
    return pltpu.CompilerParams(dimension_semantics=sem, vmem_limit_bytes=vmem)


def _rms(x, g):
    return x * lax.rsqrt(jnp.mean(x * x, axis=-1, keepdims=True) + RMS_EPS) * g


def _softmax_heads(s3, bias, maskf):
    s = s3 + bias[None]
    m = jnp.max(s, axis=-1, keepdims=True)
    e = jnp.exp(s - m) * maskf[None]
    return e / jnp.maximum(jnp.sum(e, axis=-1, keepdims=True), TINY)


def _norm_matmul_kernel(x_ref, g_ref, w_ref, o_ref, xn_ref):
    @pl.when(pl.program_id(1) == 0)
    def _():
        xn_ref[...] = _rms(x_ref[...], g_ref[...]).astype(BF16)

    o_ref[...] = jnp.dot(xn_ref[...], w_ref[...], preferred_element_type=F32)


def norm_matmul(x, g, w, tm, tn):
    n, d = x.shape
    c = w.shape[1]
    return pl.pallas_call(
        _norm_matmul_kernel,
        out_shape=jax.ShapeDtypeStruct((n, c), F32),
        grid=(n // tm, c // tn),
        in_specs=[pl.BlockSpec((tm, d), lambda i, j: (i, 0)),
                  pl.BlockSpec((1, d), lambda i, j: (0, 0)),
                  pl.BlockSpec((d, tn), lambda i, j: (0, j))],
        out_specs=pl.BlockSpec((tm, tn), lambda i, j: (i, j)),
        scratch_shapes=[pltpu.VMEM((tm, d), BF16)],
        compiler_params=_cparams(("parallel", "arbitrary")),
        name="norm_in_proj",
    )(x, g.reshape(1, d), w)


def _rope_half(x, cos, sin):
    return x * cos + pltpu.roll(x, 64, 1) * sin


def _rope_quarter(x, cos, sin):
    lane = lax.broadcasted_iota(I32, x.shape, 1)
    partner = jnp.where((lane % 64) < 32, pltpu.roll(x, 96, 1), pltpu.roll(x, 32, 1))
    return x * cos + partner * sin


def _split_kernel(z_ref, c128_ref, s128_ref, c64_ref, s64_ref, gq_ref, gkv_ref, wuq_ref, wuk_ref,
                  q_ref, kvc_ref, kvs_ref, kvw_ref, kvsb_ref, kvwb_ref, gate_ref, qcat_ref, row_ref, rowb_ref):
    c128, s128 = c128_ref[...], s128_ref[...]
    c64, s64 = c64_ref[...], s64_ref[...]
    for h in range(NSA_HEADS):
        x = z_ref[:, ZC_Q + h * 128:ZC_Q + (h + 1) * 128]
        q_ref[:, h * 128:(h + 1) * 128] = (_rope_half(x, c128, s128) * NSA_SCALE).astype(BF16)
    for br, (o32, o16) in enumerate(((kvc_ref, None), (kvs_ref, kvsb_ref), (kvw_ref, kvwb_ref))):
        c0 = ZC_KV + br * 256
        k = _rope_half(z_ref[:, c0:c0 + 128], c128, s128)
        v = z_ref[:, c0 + 128:c0 + 256]
        o32[pl.ds(0, k.shape[0], stride=2), :] = k
        o32[pl.ds(1, k.shape[0], stride=2), :] = v
        if o16 is not None:
            o16[:, 0:128] = k.astype(BF16)
            o16[:, 128:256] = v.astype(BF16)
    gate_ref[...] = jax.nn.sigmoid(z_ref[:, ZC_GATE:ZC_GATE + LANES])

    zc = z_ref[:, ZC_MLA:ZC_MLA + MLA_ROW]
    lane384 = lax.broadcasted_iota(I32, zc.shape, 1)
    sq = jnp.where(lane384 < MLA_KV_LORA, zc * zc, 0.0)
    rstd = lax.rsqrt(jnp.sum(sq, axis=-1, keepdims=True) / MLA_KV_LORA + RMS_EPS)
    normed = zc * rstd * gkv_ref[...]
    x3 = zc[:, 256:384]
    lane128 = lax.broadcasted_iota(I32, x3.shape, 1)
    col3 = jnp.where(lane128 < 64, normed[:, 256:384], _rope_quarter(x3, c64, s64))
    row_ref[:, 0:256] = normed[:, 0:256]
    row_ref[:, 256:384] = col3
    rowb_ref[:, 0:256] = normed[:, 0:256].astype(BF16)
    rowb_ref[:, 256:384] = col3.astype(BF16)

    cq = _rms(z_ref[:, ZC_CQ:ZC_CQ + MLA_Q_LORA], gq_ref[...]).astype(BF16)
    qm = jnp.dot(cq, wuq_ref[...], preferred_element_type=F32)
    pe_base = MLA_HEADS * MLA_NOPE
    for hp in range(MLA_HEADS // 2):
        pe = _rope_quarter(qm[:, pe_base + hp * 128:pe_base + (hp + 1) * 128], c64, s64)
        for sub in range(2):
            h = 2 * hp + sub
            nope = qm[:, h * 128:(h + 1) * 128].astype(BF16)
            ql = jnp.dot(nope, wuk_ref[h], preferred_element_type=F32)
            pe_hi = pe if sub == 1 else pltpu.roll(pe, 64, 1)
            tail = ql[:, 256:384] + jnp.where(lane128 >= 64, pe_hi, 0.0)
            qcat_ref[:, h * 384:h * 384 + 256] = (ql[:, 0:256] * MLA_SCALE).astype(BF16)
            qcat_ref[:, h * 384 + 256:(h + 1) * 384] = (tail * MLA_SCALE).astype(BF16)


def split_projection(z, tabs, tab_index, gq, gkv, wuq, wuk, tm):
    n = z.shape[0]
    c128, s128, c64, s64 = tabs
    row = lambda w: pl.BlockSpec((tm, w), lambda i: (i, 0))
    tab = pl.BlockSpec((tm, LANES), lambda i: (tab_index(i), 0))
    full = lambda a: pl.BlockSpec(a.shape, lambda i: (0,) * a.ndim)
    outs = [((n, NSA_Q_COLS), BF16), ((2 * n, 128), F32), ((2 * n, 128), F32), ((2 * n, 128), F32), ((n, 256), BF16),
            ((n, 256), BF16), ((n, LANES), F32), ((n, MLA_HEADS * MLA_ROW), BF16), ((n, MLA_ROW), F32),
            ((n, MLA_ROW), BF16)]
    return pl.pallas_call(
        _split_kernel,
        out_shape=[jax.ShapeDtypeStruct(s, d) for s, d in outs],
        grid=(n // tm,),
        in_specs=[row(Z_COLS), tab, tab, tab, tab, full(gq), full(gkv), full(wuq), full(wuk)],
        out_specs=[pl.BlockSpec((tm * s[0] // n, s[1]), lambda i: (i, 0)) for s, _ in outs],
        compiler_params=_cparams(("parallel",)),
        name="split_projection",
    )(z, c128, s128, c64, s64, gq, gkv, wuq, wuk)


ROWS_PER_CMP_BLOCK = 2 * CMP_BLOCK


def _compress_blocks(load_rows, pos_ref, w_ref, kv):
    pieces = [(load_rows(2 * l + kv) + pos_ref[2 * l + kv:2 * l + kv + 1, :]).astype(BF16) for l in range(CMP_BLOCK)]
    return jnp.dot(jnp.concatenate(pieces, axis=1), w_ref[kv], preferred_element_type=F32)


def _compress_kernel(x_ref, pos_ref, w_ref, o_ref):
    nblk = o_ref.shape[0]
    for kv in range(2):
        o_ref[:, kv * 128:(kv + 1) * 128] = _compress_blocks(
            lambda r: x_ref[pl.ds(r, nblk, stride=ROWS_PER_CMP_BLOCK), :], pos_ref, w_ref, kv)


def compress_rows(x, pos, w, nblk):
    rows = x.shape[0]
    step = nblk * ROWS_PER_CMP_BLOCK
    return pl.pallas_call(
        _compress_kernel,
        out_shape=jax.ShapeDtypeStruct((rows // ROWS_PER_CMP_BLOCK, 256), F32),
        grid=(rows // step,),
        in_specs=[pl.BlockSpec((step, 128), lambda i: (i, 0)),
                  pl.BlockSpec(pos.shape, lambda i: (0, 0)),
                  pl.BlockSpec(w.shape, lambda i: (0, 0, 0))],
        out_specs=pl.BlockSpec((nblk, 256), lambda i: (i, 0)),
        compiler_params=_cparams(("parallel",)),
        name="compress_prompt",
    )(x, pos, w)


def _nsa_prompt_kernel(q_ref, g_ref, ck_ref, ks_ref, kw_ref, o_ref, *, seq, kc):
    H, QB = NSA_HEADS, QUERY_BLOCK
    n_sel = seq // SEL_BLOCK
    half = ck_ref.shape[1] // 2
    qb = pl.program_id(1)
    start = qb * QB
    Q = jnp.concatenate([q_ref[:, h * 128:(h + 1) * 128] for h in range(H)], axis=0)
    qpos = start + lax.broadcasted_iota(I32, (QB, 1), 0)

    ck = ck_ref[0]
    s_c = lax.dot_general(Q, ck[:, :128].astype(BF16), _CONTRACT_LAST, preferred_element_type=F32)
    lane = lax.broadcasted_iota(I32, (1, 2 * half), 1)
    jj = lane % half
    blk = 2 * jj + (lane >= half).astype(I32)
    mask_c = (((blk + 1) * CMP_BLOCK - 1) <= qpos) & (jj < n_sel)
    p_c = _softmax_heads(s_c.reshape(H, QB, 2 * half), jnp.where(mask_c, 0.0, NEG_INF), mask_c.astype(F32))
    o_c = jnp.dot(p_c.reshape(H * QB, 2 * half).astype(BF16), ck[:, 128:].astype(BF16),
                  preferred_element_type=F32)

    imp = jnp.sum(p_c, axis=0)
    imp = imp[:, :half] + imp[:, half:]
    j = lax.broadcasted_iota(I32, (1, half), 1)
    cur = qpos // SEL_BLOCK
    valid = (j * SEL_BLOCK <= qpos) & (j < n_sel)
    forced = (j == 0) | (j == cur) | (j == cur - 1)
    score = jnp.where(valid, imp + jnp.where(forced, FORCE_BONUS, 0.0), -jnp.inf)
    rank = jnp.zeros((QB, half), F32)
    for i in range(n_sel):
        col = score[:, i:i + 1]
        ahead = (col > score) | ((col == score) & (j > i))
        rank = rank + ahead.astype(F32)
    chosen = ((rank < min(N_SELECT, n_sel)) & valid).astype(BF16)
    jcol = lax.broadcasted_iota(I32, (half, 1), 0)

    def sel_body(c, carry):
        m, l, acc = carry
        k0 = pl.multiple_of(c * kc, kc)
        kk = ks_ref[pl.ds(k0, kc), 0:128]
        vv = ks_ref[pl.ds(k0, kc), 128:256]
        s = lax.dot_general(Q, kk, _CONTRACT_LAST, preferred_element_type=F32).reshape(H, QB, kc)
        kidx = k0 + lax.broadcasted_iota(I32, (1, kc), 1)
        expand = ((kidx // SEL_BLOCK) == jcol).astype(BF16)
        picked = jnp.dot(chosen, expand, preferred_element_type=F32) > 0.5
        ok = picked & (kidx <= qpos)
        s = s + jnp.where(ok, 0.0, NEG_INF)[None]
        m_new = jnp.maximum(m, jnp.max(s, axis=-1, keepdims=True))
        alpha = jnp.exp(m - m_new)
        e = jnp.exp(s - m_new) * ok.astype(F32)[None]
        l = alpha * l + jnp.sum(e, axis=-1, keepdims=True)
        pv = jnp.dot(e.reshape(H * QB, kc).astype(BF16), vv, preferred_element_type=F32)
        acc = alpha.reshape(H * QB, 1) * acc + pv
        return m_new, l, acc

    n_chunks = (start + QB + kc - 1) // kc
    m0 = jnp.full((H, QB, 1), NEG_INF, F32)
    l0 = jnp.zeros((H, QB, 1), F32)
    a0 = jnp.zeros((H * QB, 128), F32)
    _, l_s, acc_s = lax.fori_loop(0, n_chunks, sel_body, (m0, l0, a0))
    o_s = acc_s / jnp.maximum(l_s, TINY).reshape(H * QB, 1)

    wk = WINDOW + QB
    w0 = pl.multiple_of(jnp.clip(start - WINDOW, 0, seq - wk), QB)
    kk = kw_ref[pl.ds(w0, wk), 0:128]
    vv = kw_ref[pl.ds(w0, wk), 128:256]
    s_w = lax.dot_general(Q, kk, _CONTRACT_LAST, preferred_element_type=F32).reshape(H, QB, wk)
    kidx = w0 + lax.broadcasted_iota(I32, (1, wk), 1)
    mask_w = (kidx <= qpos) & (kidx > qpos - WINDOW)
    p_w = _softmax_heads(s_w, jnp.where(mask_w, 0.0, NEG_INF), mask_w.astype(F32))
    o_w = jnp.dot(p_w.reshape(H * QB, wk).astype(BF16), vv, preferred_element_type=F32)

    g = g_ref[...]
    for h in range(H):
        r = slice(h * QB, (h + 1) * QB)
        o_ref[:, h * 128:(h + 1) * 128] = (g[:, h:h + 1] * o_c[r] + g[:, H + h:H + h + 1] * o_s[r]
                                           + g[:, 2 * H + h:2 * H + h + 1] * o_w[r])


def nsa_prompt(q, gates, ck, kvs_b, kvw_b, batch, seq, kc=512):
    nqb = seq // QUERY_BLOCK
    assert seq % kc == 0 and seq >= WINDOW + QUERY_BLOCK
    return pl.pallas_call(
        functools.partial(_nsa_prompt_kernel, seq=seq, kc=kc),
        out_shape=jax.ShapeDtypeStruct((batch * seq, GROUP_WIDTH), F32),
        grid=(batch, nqb),
        in_specs=[pl.BlockSpec((QUERY_BLOCK, GROUP_WIDTH), lambda b, i: (b * nqb + i, 0)),
                  pl.BlockSpec((QUERY_BLOCK, LANES), lambda b, i: (b * nqb + i, 0)),
                  pl.BlockSpec((1,) + ck.shape[1:], lambda b, i: (b, 0, 0)),
                  pl.BlockSpec((seq, 256), lambda b, i: (b, 0)),
                  pl.BlockSpec((seq, 256), lambda b, i: (b, 0))],
        out_specs=pl.BlockSpec((QUERY_BLOCK, GROUP_WIDTH), lambda b, i: (b * nqb + i, 0)),
        compiler_params=_cparams(("parallel", "parallel")),
        name="nsa_prompt",
    )(q, gates, ck, kvs_b, kvw_b)


def _mla_prompt_kernel(q_ref, rows_ref, o_ref, *, kc):
    H, QB = MLA_HEADS, QUERY_BLOCK
    start = pl.program_id(1) * QB
    Q = jnp.concatenate([q_ref[:, h * MLA_ROW:(h + 1) * MLA_ROW] for h in range(H)], axis=0)
    qpos = start + lax.broadcasted_iota(I32, (QB, 1), 0)

    def body(c, carry):
        m, l, acc = carry
        k0 = pl.multiple_of(c * kc, kc)
        rows = rows_ref[pl.ds(k0, kc), :]
        s = lax.dot_general(Q, rows, _CONTRACT_LAST, preferred_element_type=F32).reshape(H, QB, kc)
        ok = (k0 + lax.broadcasted_iota(I32, (1, kc), 1)) <= qpos
        s = s + jnp.where(ok, 0.0, NEG_INF)[None]
        m_new = jnp.maximum(m, jnp.max(s, axis=-1, keepdims=True))
        alpha = jnp.exp(m - m_new)
        e = jnp.exp(s - m_new) * ok.astype(F32)[None]
        l = alpha * l + jnp.sum(e, axis=-1, keepdims=True)
        pv = jnp.dot(e.reshape(H * QB, kc).astype(BF16), rows, preferred_element_type=F32)
        acc = alpha.reshape(H * QB, 1) * acc + pv
        return m_new, l, acc

    n_chunks = (start + QB + kc - 1) // kc
    m0 = jnp.full((H, QB, 1), NEG_INF, F32)
    l0 = jnp.zeros((H, QB, 1), F32)
    a0 = jnp.zeros((H * QB, MLA_ROW), F32)
    _, l, acc = lax.fori_loop(0, n_chunks, body, (m0, l0, a0))
    o = (acc / jnp.maximum(l, TINY).reshape(H * QB, 1)).astype(BF16)
    for h in range(H):
        o_ref[:, h * MLA_ROW:(h + 1) * MLA_ROW] = o[h * QB:(h + 1) * QB]


def mla_prompt(qcat, rows_b, batch, seq, kc=512):
    nqb = seq // QUERY_BLOCK
    width = MLA_HEADS * MLA_ROW
    return pl.pallas_call(
        functools.partial(_mla_prompt_kernel, kc=kc),
        out_shape=jax.ShapeDtypeStruct((batch * seq, width), BF16),
        grid=(batch, nqb),
        in_specs=[pl.BlockSpec((QUERY_BLOCK, width), lambda b, i: (b * nqb + i, 0)),
                  pl.BlockSpec((seq, MLA_ROW), lambda b, i: (b, 0))],
        out_specs=pl.BlockSpec((QUERY_BLOCK, width), lambda b, i: (b * nqb + i, 0)),
        compiler_params=_cparams(("parallel", "parallel")),
        name="mla_prompt",
    )(qcat, rows_b)


def _gather_pipeline(pt_ref, page_copies, ch):
    b, c = pl.program_id(0), pl.program_id(1)
    nc = pl.num_programs(1)
    t = b * nc + c
    total = pl.num_programs(0) * nc
    slot = t % 2

    def start(tb, tc, sl):
        for p in range(ch):
            for cp in page_copies(pt_ref[tb, tc * ch + p], p, sl):
                cp.start()

    @pl.when(t == 0)
    def _():
        start(0, 0, 0)

    @pl.when(t + 1 < total)
    def _():
        start((t + 1) // nc, (t + 1) % nc, 1 - slot)

    for p in range(ch):
        for cp in page_copies(0, p, slot):
            cp.wait()
    return slot


def _cmp_sample_kernel(pt_ref, q_ref, w_ref, pos_ref, cache_ref, oc_ref, imp_ref,
                       buf, sem, ck_scr, *, ch, past, dec_seq):
    c = pl.program_id(1)
    blocks_per_page = PAGE_SIZE // CMP_BLOCK
    nsb = ch * blocks_per_page // 2

    def page_copies(page, p, sl):
        copies = []
        for blk in range(blocks_per_page):
            col = (blk % 2) * nsb + p * (blocks_per_page // 2) + blk // 2
            row0 = pl.multiple_of((page * blocks_per_page + blk) * ROWS_PER_CMP_BLOCK, ROWS_PER_CMP_BLOCK)
            copies.append(pltpu.make_async_copy(cache_ref.at[pl.ds(row0, ROWS_PER_CMP_BLOCK)],
                                                buf.at[sl, :, pl.ds(col, 1), :], sem.at[sl]))
        return copies

    slot = _gather_pipeline(pt_ref, page_copies, ch)
    r0 = pl.multiple_of(c * nsb, nsb)
    for kv in range(2):
        ckp = _compress_blocks(lambda r: buf[slot, r], pos_ref, w_ref, kv)
        ck_scr[0, kv, pl.ds(r0, nsb), :] = ckp[:nsb]
        ck_scr[1, kv, pl.ds(r0, nsb), :] = ckp[nsb:]

    @pl.when(c == pl.num_programs(1) - 1)
    def _():
        H = NSA_HEADS
        nsel = ck_scr.shape[2]
        ckk = jnp.concatenate([ck_scr[0, 0], ck_scr[1, 0]], axis=0).astype(BF16)
        ckv = jnp.concatenate([ck_scr[0, 1], ck_scr[1, 1]], axis=0).astype(BF16)
        q = q_ref[0]
        s = lax.dot_general(q, ckk, _CONTRACT_LAST, preferred_element_type=F32)
        lane = lax.broadcasted_iota(I32, (1, 2 * nsel), 1)
        blk = 2 * (lane % nsel) + (lane >= nsel).astype(I32)
        qpos = past + lax.broadcasted_iota(I32, (dec_seq * H, 1), 0) // H
        mask = ((blk + 1) * CMP_BLOCK - 1) <= qpos
        s = jnp.where(mask, s, NEG_INF)
        m = jnp.max(s, axis=-1, keepdims=True)
        e = jnp.where(mask, jnp.exp(s - m), 0.0)
        p = e / jnp.maximum(jnp.sum(e, axis=-1, keepdims=True), TINY)
        oc_ref[0] = jnp.dot(p.astype(BF16), ckv, preferred_element_type=F32)
        imp = jnp.sum(p.reshape(dec_seq, H, 2 * nsel), axis=1)
        imp_ref[0] = imp[:, :nsel] + imp[:, nsel:]


def cmp_sample(page_table, q_s, w_cmp, pos_cmp, cache_rows, dec_seq, ch):
    db, n_pages = page_table.shape
    nsel = 2 * n_pages
    rows = dec_seq * NSA_HEADS
    grid_spec = pltpu.PrefetchScalarGridSpec(
        num_scalar_prefetch=1,
        grid=(db, n_pages // ch),
        in_specs=[pl.BlockSpec((1, rows, 128), lambda b, c, pt: (b, 0, 0)),
                  pl.BlockSpec(w_cmp.shape, lambda b, c, pt: (0, 0, 0)),
                  pl.BlockSpec(pos_cmp.shape, lambda b, c, pt: (0, 0)),
                  pl.BlockSpec(memory_space=pl.ANY)],
        out_specs=[pl.BlockSpec((1, rows, 128), lambda b, c, pt: (b, 0, 0)),
                   pl.BlockSpec((1, dec_seq, nsel), lambda b, c, pt: (b, 0, 0))],
        scratch_shapes=[pltpu.VMEM((2, ROWS_PER_CMP_BLOCK, ch * PAGE_SIZE // CMP_BLOCK, 128), F32),
                        pltpu.SemaphoreType.DMA((2,)),
                        pltpu.VMEM((2, 2, nsel, 128), F32)])
    return pl.pallas_call(
        functools.partial(_cmp_sample_kernel, ch=ch, past=n_pages * PAGE_SIZE, dec_seq=dec_seq),
        out_shape=[jax.ShapeDtypeStruct((db, rows, 128), F32), jax.ShapeDtypeStruct((db, dec_seq, nsel), F32)],
        grid_spec=grid_spec,
        compiler_params=_cparams(("arbitrary", "arbitrary")),
        name="cmp_sample",
    )(page_table, q_s, w_cmp, pos_cmp, cache_rows)


def _select_kernel(imp_ref, idx_ref, val_ref, *, past, dec_seq, n_sel):
    nq, n_past = imp_ref.shape
    x = jnp.concatenate([imp_ref[...], jnp.zeros((nq, LANES), F32)], axis=1)
    width = n_past + LANES
    j = lax.broadcasted_iota(I32, (1, width), 1)
    jf = j.astype(F32)
    qpos = past + lax.broadcasted_iota(I32, (nq, 1), 0) % dec_seq
    cur = qpos // SEL_BLOCK
    valid = (j * SEL_BLOCK <= qpos) & (j < n_sel)
    forced = (j == 0) | (j == cur) | (j == cur - 1)
    score = jnp.where(valid, x + jnp.where(forced, FORCE_BONUS, 0.0), -jnp.inf)
    out_lane = lax.broadcasted_iota(I32, (1, LANES), 1)
    idx = jnp.zeros((nq, LANES), F32)
    val = jnp.zeros((nq, LANES), F32)
    for r in range(min(N_SELECT, n_sel)):
        m = jnp.max(score, axis=-1, keepdims=True)
        ix = jnp.min(jnp.where(score == m, jf, float(width)), axis=-1, keepdims=True)
        idx = jnp.where(out_lane == r, ix, idx)
        val = jnp.where(out_lane == r, (m > -jnp.inf).astype(F32), val)
        score = jnp.where(jf == ix, -jnp.inf, score)
    idx_ref[...] = idx.astype(I32)
    val_ref[...] = val.astype(I32)


def select_sample(imp, past, dec_seq, n_sel):
    nq = imp.shape[0]
    return pl.pallas_call(
        functools.partial(_select_kernel, past=past, dec_seq=dec_seq, n_sel=n_sel),
        out_shape=[jax.ShapeDtypeStruct((nq, LANES), I32), jax.ShapeDtypeStruct((nq, LANES), I32)],
        compiler_params=_cparams(None),
        name="select_sample",
    )(imp)


ROWS_PER_SEL_BLOCK = 2 * SEL_BLOCK


def _sel_copy(cache_ref, new_ref, pt_ref, idx_ref, b, e, buf, slot, sem, n_pages):
    n_past_blk = 2 * n_pages
    ix = idx_ref[b, e]
    dst = buf.at[slot, pl.ds(e * ROWS_PER_SEL_BLOCK, ROWS_PER_SEL_BLOCK), :]
    is_new = ix >= n_past_blk
    page = pt_ref[b, jnp.minimum(ix // 2, n_pages - 1)]
    row0 = pl.multiple_of((2 * page + ix % 2) * ROWS_PER_SEL_BLOCK, ROWS_PER_SEL_BLOCK)

    def new_copy():
        return pltpu.make_async_copy(new_ref.at[b], dst, sem.at[slot])

    def past_copy():
        return pltpu.make_async_copy(cache_ref.at[pl.ds(row0, ROWS_PER_SEL_BLOCK), :], dst, sem.at[slot])

    return is_new, new_copy, past_copy


def _masked_softmax(s, ok):
    s = jnp.where(ok, s, NEG_INF)
    m = jnp.max(s, axis=-1, keepdims=True)
    e = jnp.where(ok, jnp.exp(s - m), 0.0)
    return e / jnp.maximum(jnp.sum(e, axis=-1, keepdims=True), TINY)


def _sel_sample_kernel(pt_ref, idx_ref, q_ref, ok_ref, oc_ref, g_ref, win_ref, wnew_ref, wmask_ref,
                       cache_ref, new_ref, o_ref, buf, sem, kw_scr, *, dec_seq, n_pages):
    H = NSA_HEADS
    nsl = dec_seq * N_SELECT
    b = pl.program_id(0)
    nb = pl.num_programs(0)
    slot = b % 2

    def start(tb, sl):
        for e in range(nsl):
            is_new, new_copy, past_copy = _sel_copy(cache_ref, new_ref, pt_ref, idx_ref, tb, e, buf, sl, sem, n_pages)

            @pl.when(is_new)
            def _():
                new_copy().start()

            @pl.when(jnp.logical_not(is_new))
            def _():
                past_copy().start()

    @pl.when(b == 0)
    def _():
        start(0, 0)

    @pl.when(b + 1 < nb)
    def _():
        start(b + 1, 1 - slot)

    for e in range(nsl):
        pltpu.make_async_copy(new_ref.at[0], buf.at[slot, pl.ds(e * ROWS_PER_SEL_BLOCK, ROWS_PER_SEL_BLOCK), :],
                              sem.at[slot]).wait()

    q = q_ref[0]
    nk = N_SELECT * SEL_BLOCK
    keys = buf[slot, pl.ds(0, nsl * SEL_BLOCK, stride=2), :].astype(BF16).reshape(dec_seq, nk, 128)
    vals = buf[slot, pl.ds(1, nsl * SEL_BLOCK, stride=2), :].astype(BF16).reshape(dec_seq, nk, 128)
    q3 = q.reshape(dec_seq, H, 128)
    s = jnp.einsum('shd,skd->shk', q3, keys, preferred_element_type=F32)
    p = _masked_softmax(s, ok_ref[0][:, None, :] > 0.5)
    o_s = jnp.einsum('shk,skd->shd', p.astype(BF16), vals, preferred_element_type=F32).reshape(dec_seq * H, 128)

    wb = win_ref.shape[1] // 2
    nw = wnew_ref.shape[1] // 2
    for kv in range(2):
        kw_scr[kv, 0:wb, :] = win_ref[0, pl.ds(kv, wb, stride=2), :]
        kw_scr[kv, wb:, :] = jnp.zeros((kw_scr.shape[1] - wb, 128), F32)
        kw_scr[kv, wb:wb + nw, :] = wnew_ref[0, pl.ds(kv, nw, stride=2), :]
    s = lax.dot_general(q, kw_scr[0].astype(BF16), _CONTRACT_LAST, preferred_element_type=F32)
    p = _masked_softmax(s, wmask_ref[...] > 0.5)
    o_w = jnp.dot(p.astype(BF16), kw_scr[1].astype(BF16), preferred_element_type=F32)

    g = g_ref[0]
    o_ref[0] = g[:, 0:1] * oc_ref[0] + g[:, 1:2] * o_s + g[:, 2:3] * o_w


def sel_sample(page_table, idx, q_s, okmask, o_c, gates_rows, win, wnew, wmask, cache_sel, new_blocks, dec_seq):
    db, n_pages = page_table.shape
    rows = dec_seq * NSA_HEADS
    nkeys = N_SELECT * SEL_BLOCK
    wk = wmask.shape[1]
    b3 = lambda shape: pl.BlockSpec((1,) + shape, lambda b, pt, ix: (b, 0, 0))
    grid_spec = pltpu.PrefetchScalarGridSpec(
        num_scalar_prefetch=2,
        grid=(db,),
        in_specs=[b3((rows, 128)), b3((dec_seq, nkeys)), b3((rows, 128)), b3((rows, 3)),
                  b3(win.shape[1:]), b3(wnew.shape[1:]),
                  pl.BlockSpec(wmask.shape, lambda b, pt, ix: (0, 0)),
                  pl.BlockSpec(memory_space=pl.ANY), pl.BlockSpec(memory_space=pl.ANY)],
        out_specs=b3((rows, 128)),
        scratch_shapes=[pltpu.VMEM((2, dec_seq * N_SELECT * ROWS_PER_SEL_BLOCK, 128), F32),
                        pltpu.SemaphoreType.DMA((2,)),
                        pltpu.VMEM((2, wk, 128), F32)])
    return pl.pallas_call(
        functools.partial(_sel_sample_kernel, dec_seq=dec_seq, n_pages=n_pages),
        out_shape=jax.ShapeDtypeStruct((db, rows, 128), F32),
        grid_spec=grid_spec,
        compiler_params=_cparams(("arbitrary",)),
        name="sel_win_sample",
    )(page_table, idx, q_s, okmask, o_c, gates_rows, win, wnew, wmask, cache_sel, new_blocks)


def _mla_sample_kernel(pt_ref, q_ref, new_ref, cache_ref, o_ref, buf, sem, m_scr, l_scr, acc_scr, new_scr,
                       *, ch, dec_seq):
    H = MLA_HEADS
    c = pl.program_id(1)
    def page_copies(page, p, sl):
        return [pltpu.make_async_copy(cache_ref.at[page], buf.at[sl, pl.ds(p * PAGE_SIZE, PAGE_SIZE), :], sem.at[sl])]

    slot = _gather_pipeline(pt_ref, page_copies, ch)
    q = q_ref[0]

    @pl.when(c == 0)
    def _():
        m_scr[...] = jnp.full(m_scr.shape, NEG_INF, F32)
        l_scr[...] = jnp.zeros(l_scr.shape, F32)
        acc_scr[...] = jnp.zeros(acc_scr.shape, F32)

    def update(rows, ok):
        s = lax.dot_general(q, rows, _CONTRACT_LAST, preferred_element_type=F32)
        if ok is not None:
            s = jnp.where(ok, s, NEG_INF)
        m_old = m_scr[...]
        m_new = jnp.maximum(m_old, jnp.max(s, axis=-1, keepdims=True))
        alpha = jnp.exp(m_old - m_new)
        e = jnp.exp(s - m_new)
        if ok is not None:
            e = jnp.where(ok, e, 0.0)
        l_scr[...] = alpha * l_scr[...] + jnp.sum(e, axis=-1, keepdims=True)
        acc_scr[...] = alpha * acc_scr[...] + jnp.dot(e.astype(BF16), rows, preferred_element_type=F32)
        m_scr[...] = m_new

    update(buf[slot].astype(BF16), None)

    @pl.when(c == pl.num_programs(1) - 1)
    def _():
        nr = new_ref.shape[1]
        new_scr[...] = jnp.zeros(new_scr.shape, BF16)
        new_scr[0:nr, :] = new_ref[0]
        t = lax.broadcasted_iota(I32, (1, new_scr.shape[0]), 1)
        srow = lax.broadcasted_iota(I32, (dec_seq * H, 1), 0) // H
        update(new_scr[...], (t <= srow) & (t < dec_seq))
        o_ref[0] = (acc_scr[...] / jnp.maximum(l_scr[...], TINY)).astype(BF16)


def mla_sample(page_table, qcat_s, new_rows, cache, dec_seq, ch):
    db, n_pages = page_table.shape
    rows = dec_seq * MLA_HEADS
    grid_spec = pltpu.PrefetchScalarGridSpec(
        num_scalar_prefetch=1,
        grid=(db, n_pages // ch),
        in_specs=[pl.BlockSpec((1, rows, MLA_ROW), lambda b, c, pt: (b, 0, 0)),
                  pl.BlockSpec((1,) + new_rows.shape[1:], lambda b, c, pt: (b, 0, 0)),
                  pl.BlockSpec(memory_space=pl.ANY)],
        out_specs=pl.BlockSpec((1, rows, MLA_ROW), lambda b, c, pt: (b, 0, 0)),
        scratch_shapes=[pltpu.VMEM((2, ch * PAGE_SIZE, MLA_ROW), F32),
                        pltpu.SemaphoreType.DMA((2,)),
                        pltpu.VMEM((rows, 1), F32), pltpu.VMEM((rows, 1), F32),
                        pltpu.VMEM((rows, MLA_ROW), F32), pltpu.VMEM((LANES, MLA_ROW), BF16)])
    return pl.pallas_call(
        functools.partial(_mla_sample_kernel, ch=ch, dec_seq=dec_seq),
        out_shape=jax.ShapeDtypeStruct((db, rows, MLA_ROW), BF16),
        grid_spec=grid_spec,
        compiler_params=_cparams(("arbitrary", "arbitrary")),
        name="mla_sample",
    )(page_table, qcat_s, new_rows, cache)


def _head_proj_kernel(x_ref, w_ref, o_ref):
    for h in range(MLA_HEADS):
        o_ref[:, h * MLA_V:(h + 1) * MLA_V] = jnp.dot(x_ref[:, h * MLA_ROW:(h + 1) * MLA_ROW], w_ref[h],
                                                      preferred_element_type=F32)


def head_proj(x, w, tm):
    n = x.shape[0]
    return pl.pallas_call(
        _head_proj_kernel,
        out_shape=jax.ShapeDtypeStruct((n, GROUP_WIDTH), F32),
        grid=(n // tm,),
        in_specs=[pl.BlockSpec((tm, x.shape[1]), lambda i: (i, 0)),
                  pl.BlockSpec(w.shape, lambda i: (0, 0, 0))],
        out_specs=pl.BlockSpec((tm, GROUP_WIDTH), lambda i: (i, 0)),
        compiler_params=_cparams(("parallel",)),
        name="mla_value_proj",
    )(x, w)


def _out_proj_kernel(a_ref, b_ref, ga_ref, gb_ref, w_ref, h_ref, o_ref, cat_ref):
    @pl.when(pl.program_id(1) == 0)
    def _():
        cat_ref[:, 0:GROUP_WIDTH] = _rms(a_ref[...], ga_ref[...]).astype(BF16)
        cat_ref[:, GROUP_WIDTH:2 * GROUP_WIDTH] = _rms(b_ref[...], gb_ref[...]).astype(BF16)

    o_ref[...] = h_ref[...] + jnp.dot(cat_ref[...], w_ref[...], preferred_element_type=F32)


def out_proj(o_nsa, o_mla, ga, gb, w, h, tm, tn):
    n, d = h.shape
    gw = GROUP_WIDTH
    return pl.pallas_call(
        _out_proj_kernel,
        out_shape=jax.ShapeDtypeStruct((n, d), F32),
        grid=(n // tm, d // tn),
        in_specs=[pl.BlockSpec((tm, gw), lambda i, j: (i, 0)),
                  pl.BlockSpec((tm, gw), lambda i, j: (i, 0)),
                  pl.BlockSpec((1, gw), lambda i, j: (0, 0)),
                  pl.BlockSpec((1, gw), lambda i, j: (0, 0)),
                  pl.BlockSpec((2 * gw, tn), lambda i, j: (0, j)),
                  pl.BlockSpec((tm, tn), lambda i, j: (i, j))],
        out_specs=pl.BlockSpec((tm, tn), lambda i, j: (i, j)),
        scratch_shapes=[pltpu.VMEM((tm, 2 * gw), BF16)],
        compiler_params=_cparams(("parallel", "arbitrary")),
        name="out_proj",
    )(o_nsa, o_mla, ga, gb, w, h)


def _router_kernel(h_ref, g_ref, w_ref, b_ref, xn_ref, ids_ref, gts_ref):
    xn = _rms(h_ref[...], g_ref[...])
    xn_ref[...] = xn
    logits = jnp.dot(xn, w_ref[...], preferred_element_type=F32, precision=lax.Precision.HIGHEST) + b_ref[...]
    lane = lax.broadcasted_iota(I32, logits.shape, 1)
    lanef = lane.astype(F32)
    big = float(LANES)

    def first_argmax(v):
        m = jnp.max(v, axis=-1, keepdims=True)
        return m, jnp.min(jnp.where(v == m, lanef, big), axis=-1, keepdims=True)

    gl = jnp.where(lane < N_GROUPS, logits, -jnp.inf)
    gmax, gidx = first_argmax(gl)
    g_gate = 1.0 / jnp.sum(jnp.exp(gl - gmax), axis=-1, keepdims=True)
    group_of_lane = ((lane - N_GROUPS) // EXPERTS_PER_GROUP).astype(F32)
    in_group = (lane >= N_GROUPS) & (lane < N_GROUPS + N_EXPERTS) & (group_of_lane == gidx)
    el = jnp.where(in_group, logits, -jnp.inf)
    v0, i0 = first_argmax(el)
    v1, i1 = first_argmax(jnp.where(lanef == i0, -jnp.inf, el))
    e0 = jnp.exp(v0 - v0)
    e1 = jnp.exp(v1 - v0)
    den = e0 + e1
    ids = jnp.where(lane == 0, i0 - N_GROUPS, jnp.where(lane == 1, i1 - N_GROUPS, 0.0))
    gts = jnp.where(lane == 0, g_gate * (e0 / den), jnp.where(lane == 1, g_gate * (e1 / den), 0.0))
    ids_ref[...] = ids.astype(I32)
    gts_ref[...] = gts


def router(h, g, w, b, tm):
    n, d = h.shape
    return pl.pallas_call(
        _router_kernel,
        out_shape=[jax.ShapeDtypeStruct((n, d), F32), jax.ShapeDtypeStruct((n, LANES), I32),
                   jax.ShapeDtypeStruct((n, LANES), F32)],
        grid=(n // tm,),
        in_specs=[pl.BlockSpec((tm, d), lambda i: (i, 0)),
                  pl.BlockSpec((1, d), lambda i: (0, 0)),
                  pl.BlockSpec((d, LANES), lambda i: (0, 0)),
                  pl.BlockSpec((1, LANES), lambda i: (0, 0))],
        out_specs=[pl.BlockSpec((tm, d), lambda i: (i, 0)),
                   pl.BlockSpec((tm, LANES), lambda i: (i, 0)),
                   pl.BlockSpec((tm, LANES), lambda i: (i, 0))],
        compiler_params=_cparams(("parallel",)),
        name="ffn_norm_router",
    )(h, g.reshape(1, d), w, b)


ROW_DMA_WAIT_GROUP = 8
FFN_CAST_CHUNK = 512


def _ffn_kernel(be_ref, rs_ref, cnt_ref, nu_ref, order_ref, x_hbm, wg_ref, wu_ref, wd_ref, y_hbm,
                xf_ref, xb_ref, acc_ref, gsem, ssem, *, n_tok):
    i, f = pl.program_id(0), pl.program_id(1)
    nf = pl.num_programs(1)
    nu = nu_ref[0]
    grp = ROW_DMA_WAIT_GROUP

    def gather_row(base, r):
        tok = order_ref[base + r] // 2
        return pltpu.make_async_copy(x_hbm.at[pl.ds(tok, 1), :], xf_ref.at[pl.ds(r, 1), :], gsem)

    def scatter_row(base, r):
        entry = order_ref[base + r]
        dst = (entry % 2) * n_tok + entry // 2
        return pltpu.make_async_copy(acc_ref.at[pl.ds(r, 1), :], y_hbm.at[pl.ds(dst, 1), :], ssem)

    def start_rows(blk, row_copy):
        n, base = cnt_ref[blk], rs_ref[blk]

        def groups(g, carry):
            for u in range(grp):
                row_copy(base, g * grp + u).start()
            return carry

        def singles(r, carry):
            row_copy(base, (n // grp) * grp + r).start()
            return carry
        lax.fori_loop(0, n // grp, groups, 0)
        lax.fori_loop(0, n % grp, singles, 0)

    def wait_rows(blk, row_copy, group_copy):
        n = cnt_ref[blk]

        def groups(g, carry):
            group_copy().wait()
            return carry

        def singles(r, carry):
            row_copy(0, 0).wait()
            return carry
        lax.fori_loop(0, n // grp, groups, 0)
        lax.fori_loop(0, n % grp, singles, 0)

    def gather_group():
        return pltpu.make_async_copy(x_hbm.at[pl.ds(0, grp), :], xf_ref.at[pl.ds(0, grp), :], gsem)

    def scatter_group():
        return pltpu.make_async_copy(acc_ref.at[pl.ds(0, grp), :], y_hbm.at[pl.ds(0, grp), :], ssem)

    @pl.when((i < nu) & (f == 0))
    def _():
        @pl.when(i == 0)
        def _():
            xf_ref[...] = jnp.zeros(xf_ref.shape, F32)
            acc_ref[...] = jnp.zeros(acc_ref.shape, F32)
            start_rows(0, gather_row)

        wait_rows(i, gather_row, gather_group)
        xb_ref[...] = xf_ref[...].astype(BF16)

        @pl.when(i + 1 < nu)
        def _():
            start_rows(i + 1, gather_row)

        @pl.when(i > 0)
        def _():
            wait_rows(i - 1, scatter_row, scatter_group)

    @pl.when(i < nu)
    def _():
        d = xb_ref.shape[1]
        gate = up = None
        for k0 in range(0, d, FFN_CAST_CHUNK):
            xk = xb_ref[:, k0:k0 + FFN_CAST_CHUNK]
            g_part = jnp.dot(xk, wg_ref[k0:k0 + FFN_CAST_CHUNK, :].astype(BF16), preferred_element_type=F32)
            u_part = jnp.dot(xk, wu_ref[k0:k0 + FFN_CAST_CHUNK, :].astype(BF16), preferred_element_type=F32)
            gate = g_part if gate is None else gate + g_part
            up = u_part if up is None else up + u_part
        hmid = (jax.nn.silu(gate) * up).astype(BF16)
        first = f == 0
        for n0 in range(0, d, FFN_CAST_CHUNK):
            y = jnp.dot(hmid, wd_ref[:, n0:n0 + FFN_CAST_CHUNK].astype(BF16), preferred_element_type=F32)
            acc_ref[:, n0:n0 + FFN_CAST_CHUNK] = jnp.where(first, y, acc_ref[:, n0:n0 + FFN_CAST_CHUNK] + y)

        @pl.when(f == nf - 1)
        def _():
            start_rows(i, scatter_row)

            @pl.when(i == nu - 1)
            def _():
                wait_rows(i, scatter_row, scatter_group)


def expert_ffn(block_expert, row_start, row_count, n_used, order, xn, w_gate, w_up, w_down, tm, tf):
    n_tok, d = xn.shape
    d_exp = w_gate.shape[2]
    nf = d_exp // tf
    n_blocks = block_expert.shape[0]

    def blk(i, nu):
        return jnp.minimum(i, nu[0] - 1)

    def fidx(i, f, nu):
        return jnp.where(i < nu[0], f, nf - 1)

    grid_spec = pltpu.PrefetchScalarGridSpec(
        num_scalar_prefetch=5,
        grid=(n_blocks, nf),
        in_specs=[pl.BlockSpec(memory_space=pl.ANY),
                  pl.BlockSpec((None, d, tf), lambda i, f, be, rs, cn, nu, od: (be[blk(i, nu)], 0, fidx(i, f, nu))),
                  pl.BlockSpec((None, d, tf), lambda i, f, be, rs, cn, nu, od: (be[blk(i, nu)], 0, fidx(i, f, nu))),
                  pl.BlockSpec((None, tf, d), lambda i, f, be, rs, cn, nu, od: (be[blk(i, nu)], fidx(i, f, nu), 0))],
        out_specs=pl.BlockSpec(memory_space=pl.ANY),
        scratch_shapes=[pltpu.VMEM((tm, d), F32), pltpu.VMEM((tm, d), BF16), pltpu.VMEM((tm, d), F32),
                        pltpu.SemaphoreType.DMA(()), pltpu.SemaphoreType.DMA(())])
    return pl.pallas_call(
        functools.partial(_ffn_kernel, n_tok=n_tok),
        out_shape=jax.ShapeDtypeStruct((TOPK_IN_GROUP * n_tok, d), F32),
        grid_spec=grid_spec,
        compiler_params=_cparams(("arbitrary", "arbitrary")),
        name="expert_ffn",
    )(block_expert, row_start, row_count, n_used, order, xn, w_gate, w_up, w_down)


def _combine_kernel(h_ref, y0_ref, y1_ref, g_ref, o_ref):
    g = g_ref[...]
    o_ref[...] = h_ref[...] + (g[:, 0:1] * y0_ref[...] + g[:, 1:2] * y1_ref[...])


def moe_combine(h, y, gts, tm):
    n, d = h.shape
    nt = n // tm
    return pl.pallas_call(
        _combine_kernel,
        out_shape=jax.ShapeDtypeStruct((n, d), F32),
        grid=(nt,),
        in_specs=[pl.BlockSpec((tm, d), lambda i: (i, 0)),
                  pl.BlockSpec((tm, d), lambda i: (i, 0)),
                  pl.BlockSpec((tm, d), lambda i: (i + nt, 0)),
                  pl.BlockSpec((tm, LANES), lambda i: (i, 0))],
        out_specs=pl.BlockSpec((tm, d), lambda i: (i, 0)),
        compiler_params=_cparams(("parallel",)),
        name="moe_combine",
    )(h, y, y, gts)


def _ple_kernel(h_ref, hc_ref, g_ref, wg_ref, p_ref, wp_ref, o_ref, hn_ref):
    @pl.when(pl.program_id(1) == 0)
    def _():
        hn_ref[...] = _rms(h_ref[...], g_ref[...]).astype(BF16)

    gate = jax.nn.sigmoid(jnp.dot(hn_ref[...], wg_ref[...], preferred_element_type=F32))
    proj = jnp.dot(p_ref[...], wp_ref[...], preferred_element_type=F32)
    o_ref[...] = hc_ref[...] + gate * proj


def ple_add(h, g, wg, p, wp, tm, tn):
    n, d = h.shape
    pd = p.shape[1]
    return pl.pallas_call(
        _ple_kernel,
        out_shape=jax.ShapeDtypeStruct((n, d), F32),
        grid=(n // tm, d // tn),
        in_specs=[pl.BlockSpec((tm, d), lambda i, j: (i, 0)),
                  pl.BlockSpec((tm, tn), lambda i, j: (i, j)),
                  pl.BlockSpec((1, d), lambda i, j: (0, 0)),
                  pl.BlockSpec((d, tn), lambda i, j: (0, j)),
                  pl.BlockSpec((tm, pd), lambda i, j: (i, 0)),
                  pl.BlockSpec((pd, tn), lambda i, j: (0, j))],
        out_specs=pl.BlockSpec((tm, tn), lambda i, j: (i, j)),
        scratch_shapes=[pltpu.VMEM((tm, d), BF16)],
        compiler_params=_cparams(("parallel", "arbitrary")),
        name="ple_add",
    )(h, h, g.reshape(1, d), wg, p, wp)


def _final_norm_kernel(x_ref, g_ref, o_ref):
    o_ref[...] = _rms(x_ref[...], g_ref[...])


def final_norm(x, g, tm):
    n, d = x.shape
    return pl.pallas_call(
        _final_norm_kernel,
        out_shape=jax.ShapeDtypeStruct((n, d), F32),
        grid=(n // tm,),
        in_specs=[pl.BlockSpec((tm, d), lambda i: (i, 0)), pl.BlockSpec((1, d), lambda i: (0, 0))],
        out_specs=pl.BlockSpec((tm, d), lambda i: (i, 0)),
        compiler_params=_cparams(("parallel",)),
        name="final_norm",
    )(x, g.reshape(1, d))


def _rope_tables(pos, tm):
    pos = np.asarray(pos, np.float32)[:, None]

    def table(dim):
        inv = np.float32(ROPE_THETA) ** (-np.arange(0, dim, 2, dtype=np.float32) / np.float32(dim))
        ang = (pos * inv[None, :]).astype(np.float32).astype(np.float64)
        cos, sin = np.cos(ang).astype(np.float32), np.sin(ang).astype(np.float32)
        reps = LANES // dim
        return np.tile(np.concatenate([cos, cos], 1), (1, reps)), np.tile(np.concatenate([-sin, sin], 1), (1, reps))

    c128, s128 = table(HEAD_DIM)
    c64, s64 = table(MLA_ROPE)
    return tuple(jnp.asarray(t) for t in (c128, s128, c64, s64))


def _compress_weights(cmp_pos, cmp_w):
    w = cmp_w.reshape(2, CMP_BLOCK * HEAD_DIM, HEAD_DIM).astype(BF16)
    pos = jnp.transpose(cmp_pos, (1, 0, 2)).reshape(ROWS_PER_CMP_BLOCK, HEAD_DIM)
    return w, pos


def _layer(hp, hs, p_all, cache_cmp, cache_sel, cache_mla, win_state, page_table, wts):
    (norm_attn, w_in, cmp_pos, cmp_w, mla_q_norm, mla_kv_norm, mla_w_uq, mla_w_uk, mla_w_uv, group_norm, w_out,
     norm_ffn, rg_w, rg_b, re_w, re_b, w_gate, w_up, w_down, ple_proj, ple_norm, ple_gate_w) = wts
    batch, seq, d_model = hp.shape
    db, dec_seq, _ = hs.shape
    n_pages = page_table.shape[1]
    past = n_pages * PAGE_SIZE
    n_p, n_s = batch * seq, db * dec_seq
    n = n_p + n_s
    tm = int(np.gcd(np.gcd(512, n_s), seq))
    assert n_p % tm == 0 and n_s % tm == 0 and seq % tm == 0 and tm % dec_seq == 0
    assert dec_seq < CMP_BLOCK and dec_seq <= SEL_BLOCK and past % SEL_BLOCK == 0 and past >= WINDOW
    assert win_state.shape[1] == WINDOW

    h = jnp.concatenate([hp.reshape(n_p, d_model), hs.reshape(n_s, d_model)], axis=0)

    c0, c1 = NSA_Q_COLS, NSA_Q_COLS + NSA_KV_COLS
    c2 = c1 + NSA_GATE_COLS
    w_in_p = jnp.concatenate([w_in[:, :c1], w_in[:, c2:], w_in[:, c1:c2],
                              jnp.zeros((d_model, LANES - NSA_GATE_COLS), w_in.dtype)], axis=1).astype(BF16)
    z = norm_matmul(h, norm_attn, w_in_p, tm, 384)

    tm2 = min(256, tm)
    pos_rows = np.concatenate([np.arange(seq), past + np.arange(tm2) % dec_seq])
    tabs = _rope_tables(pos_rows, tm2)
    n_ptiles, seq_tiles = n_p // tm2, seq // tm2
    tab_index = lambda i: jnp.where(i < n_ptiles, i % seq_tiles, seq_tiles)
    wuq = jnp.concatenate([mla_w_uq[:, :, :MLA_NOPE].reshape(MLA_Q_LORA, -1),
                           mla_w_uq[:, :, MLA_NOPE:].reshape(MLA_Q_LORA, -1)], axis=1).astype(BF16)
    wuk = jnp.pad(jnp.transpose(mla_w_uk, (1, 2, 0)), ((0, 0), (0, 0), (0, MLA_ROPE))).astype(BF16)
    gkv = jnp.pad(mla_kv_norm, (0, MLA_ROPE)).reshape(1, MLA_ROW)
    (q_nsa, kv_c, kv_s, kv_w, kvs_b, kvw_b, gates, qcat, mla_row, mla_row_b) = split_projection(
        z, tabs, tab_index, mla_q_norm.reshape(1, -1), gkv, wuq, wuk, tm2)

    w_cmp, pos_cmp = _compress_weights(cmp_pos, cmp_w)
    nb = seq // CMP_BLOCK
    ck = compress_rows(kv_c[:2 * n_p], pos_cmp, w_cmp, min(256, n_p // CMP_BLOCK))
    half = -(-(nb // 2) // LANES) * LANES
    ck = ck.reshape(batch, nb // 2, 2, 256)
    padh = ((0, 0), (0, half - nb // 2), (0, 0))
    ck = jnp.concatenate([jnp.pad(ck[:, :, 0], padh), jnp.pad(ck[:, :, 1], padh)], axis=1)
    o_nsa_p = nsa_prompt(q_nsa, gates, ck, kvs_b, kvw_b, batch, seq)

    o_lat_p = mla_prompt(qcat, mla_row_b, batch, seq)

    rows = dec_seq * NSA_HEADS
    q_s = q_nsa[n_p:].reshape(db, rows, HEAD_DIM)
    ch_c = min(64, n_pages)
    o_c, imp = cmp_sample(page_table, q_s, w_cmp, pos_cmp, cache_cmp.reshape(-1, 1, HEAD_DIM), dec_seq, ch_c)
    n_sel = -(-(past + dec_seq) // SEL_BLOCK)
    n_past_blk = past // SEL_BLOCK
    idx_pad, val_pad = select_sample(imp.reshape(n_s, 2 * n_pages), past, dec_seq, n_sel)
    idx = idx_pad[:, :N_SELECT]
    val = val_pad[:, :N_SELECT] > 0
    qpos_s = past + jnp.arange(n_s, dtype=I32) % dec_seq
    kpos = idx[:, :, None] * SEL_BLOCK + jnp.arange(SEL_BLOCK, dtype=I32)
    okmask = (val[:, :, None] & (kpos <= qpos_s[:, None, None])).astype(F32).reshape(db, dec_seq, N_SELECT * SEL_BLOCK)
    kvs_new = kv_s[2 * n_p:].reshape(db, 2 * dec_seq, HEAD_DIM)
    new_blocks = jnp.pad(kvs_new, ((0, 0), (0, ROWS_PER_SEL_BLOCK - 2 * dec_seq), (0, 0)))
    kvw_new = kv_w[2 * n_p:].reshape(db, 2 * dec_seq, HEAD_DIM)
    wnew = jnp.pad(kvw_new, ((0, 0), (0, -(2 * dec_seq) % 16), (0, 0)))
    wk = WINDOW + LANES
    kp = np.concatenate([past - WINDOW + np.arange(WINDOW), past + np.arange(dec_seq),
                         np.full(wk - WINDOW - dec_seq, -1)])
    qp = past + np.arange(rows) // NSA_HEADS
    wmask = ((kp[None, :] <= qp[:, None]) & (kp[None, :] > qp[:, None] - WINDOW) & (kp[None, :] >= 0))
    wmask = jnp.asarray(wmask.astype(np.float32))
    g_s = gates[n_p:, :NSA_GATE_COLS].reshape(db, dec_seq, 3, NSA_HEADS)
    g_rows = jnp.transpose(g_s, (0, 1, 3, 2)).reshape(db, rows, 3)
    win = win_state.reshape(db, 2 * WINDOW, HEAD_DIM)
    o_nsa_s = sel_sample(page_table, idx.reshape(db, dec_seq * N_SELECT), q_s, okmask, o_c, g_rows, win, wnew, wmask,
                         cache_sel.reshape(-1, HEAD_DIM), new_blocks, dec_seq)
    new_win = jnp.concatenate([win[:, 2 * dec_seq:], kvw_new], axis=1)

    qcat_s = qcat[n_p:].reshape(db, dec_seq * MLA_HEADS, MLA_ROW)
    mla_new = jnp.pad(mla_row_b[n_p:].reshape(db, dec_seq, MLA_ROW), ((0, 0), (0, 16 - dec_seq), (0, 0)))
    o_lat_s = mla_sample(page_table, qcat_s, mla_new, cache_mla, dec_seq, min(64, n_pages))

    o_nsa = jnp.concatenate([o_nsa_p, o_nsa_s.reshape(n_s, GROUP_WIDTH)], axis=0)
    o_lat = jnp.concatenate([o_lat_p, o_lat_s.reshape(n_s, MLA_HEADS * MLA_ROW)], axis=0)
    wuv = jnp.pad(jnp.transpose(mla_w_uv, (1, 0, 2)), ((0, 0), (0, MLA_ROPE), (0, 0))).astype(BF16)
    o_mla = head_proj(o_lat, wuv, tm)
    h1 = out_proj(o_nsa, o_mla, group_norm[0:1], group_norm[1:2], w_out.astype(BF16), h, tm, 512)

    w_r = jnp.concatenate([rg_w, re_w, jnp.zeros((d_model, LANES - N_GROUPS - N_EXPERTS), F32)], axis=1)
    b_r = jnp.concatenate([rg_b, re_b, jnp.zeros((LANES - N_GROUPS - N_EXPERTS,), F32)]).reshape(1, LANES)
    xn, ids, gts = router(h1, norm_ffn, w_r, b_r, tm2)
    tmf = 384
    flat_e = ids[:, :TOPK_IN_GROUP].reshape(-1)
    nk = flat_e.shape[0]
    onehot = (flat_e[:, None] == jnp.arange(N_EXPERTS, dtype=I32)[None, :]).astype(I32)
    within = jnp.sum((jnp.cumsum(onehot, axis=0) - onehot) * onehot, axis=1)
    counts = jnp.sum(onehot, axis=0)
    starts = jnp.cumsum(counts) - counts
    order = jnp.zeros((nk,), I32).at[starts[flat_e] + within].set(jnp.arange(nk, dtype=I32))
    blocks_per_expert = (counts + tmf - 1) // tmf
    blk_end = jnp.cumsum(blocks_per_expert)
    n_used = blk_end[-1].astype(I32).reshape(1)
    n_blocks = -(-(nk + N_EXPERTS * (tmf - 1)) // tmf)
    bi = jnp.arange(n_blocks, dtype=I32)
    block_expert = jnp.minimum(jnp.searchsorted(blk_end, bi, side='right'), N_EXPERTS - 1).astype(I32)
    kb = bi - (blk_end - blocks_per_expert)[block_expert]
    row_start = (starts[block_expert] + kb * tmf).astype(I32)
    row_count = jnp.where(bi < n_used[0], jnp.clip(counts[block_expert] - kb * tmf, 0, tmf), 0).astype(I32)
    row_start = jnp.where(bi < n_used[0], row_start, 0)
    y_rows = expert_ffn(block_expert, row_start, row_count, n_used, order, xn, w_gate, w_up, w_down, tmf, 256)
    h2 = moe_combine(h1, y_rows, gts, tm2)

    h3 = ple_add(h2, ple_norm, ple_gate_w.astype(BF16), p_all.astype(BF16), ple_proj.astype(BF16), tm, 512)

    caches = dict(
        cmp_p=kv_c[:2 * n_p].reshape(batch, seq, 2, HEAD_DIM), cmp_s=kv_c[2 * n_p:].reshape(db, dec_seq, 2, HEAD_DIM),
        sel_p=kv_s[:2 * n_p].reshape(batch, seq, 2, HEAD_DIM), sel_s=kv_s[2 * n_p:].reshape(db, dec_seq, 2, HEAD_DIM),
        mla_p=mla_row[:n_p].reshape(batch, seq, MLA_ROW), mla_s=mla_row[n_p:].reshape(db, dec_seq, MLA_ROW),
        win_p=kv_w[:2 * n_p].reshape(batch, 2 * seq, HEAD_DIM)[:, 2 * (seq - min(WINDOW, seq)):].reshape(
            batch, min(WINDOW, seq), 2, HEAD_DIM),
        win_s=new_win.reshape(db, WINDOW, 2, HEAD_DIM))
    return h3[:n_p].reshape(batch, seq, d_model), h3[n_p:].reshape(db, dec_seq, d_model), caches


def kernel(x_prompt, x_sample, p_prompt, p_sample, cache_nsa_cmp, cache_nsa_sel, cache_mla, state_nsa_win, page_table, norm_attn, w_in, nsa_cmp_pos, nsa_cmp_w, mla_q_norm, mla_kv_norm, mla_w_uq, mla_w_uk, mla_w_uv, group_norm, w_out, norm_ffn, router_group_w, router_group_b, router_expert_w, router_expert_b, w_gate, w_up, w_down, ple_proj, ple_norm, ple_gate_w, norm_final):
    depth = w_in.shape[0]
    hp, hs = x_prompt, x_sample
    per_layer = []
    for i in range(depth):
        ple_dim = p_prompt.shape[-1]
        p_all = jnp.concatenate([p_prompt[i].reshape(-1, ple_dim), p_sample[i].reshape(-1, ple_dim)], axis=0)
        wts = (norm_attn[i], w_in[i], nsa_cmp_pos[i], nsa_cmp_w[i], mla_q_norm[i], mla_kv_norm[i], mla_w_uq[i],
               mla_w_uk[i], mla_w_uv[i], group_norm[i], w_out[i], norm_ffn[i], router_group_w[i], router_group_b[i],
               router_expert_w[i], router_expert_b[i], w_gate[i], w_up[i], w_down[i], ple_proj[i], ple_norm[i],
               ple_gate_w[i])
        hp, hs, caches = _layer(hp, hs, p_all, cache_nsa_cmp[i], cache_nsa_sel[i], cache_mla[i], state_nsa_win[i],
                                page_table, wts)
        per_layer.append(caches)
    d_model = hp.shape[-1]
    n_p = hp.shape[0] * hp.shape[1]
    n_s = hs.shape[0] * hs.shape[1]
    y = final_norm(jnp.concatenate([hp.reshape(n_p, d_model), hs.reshape(n_s, d_model)], axis=0), norm_final,
                   int(np.gcd(np.gcd(512, n_s), n_p)))
    y_prompt = y[:n_p].reshape(hp.shape)
    y_sample = y[n_p:].reshape(hs.shape)
    stack = lambda k: jnp.stack([c[k] for c in per_layer])
    return (y_prompt, y_sample, stack('cmp_p'), stack('cmp_s'), stack('sel_p'), stack('sel_s'),
            stack('mla_p'), stack('mla_s'), stack('win_p'), stack('win_s'))
```

```python
import functools

import numpy as np
import jax
import jax.numpy as jnp
from jax import lax
from jax.experimental import pallas as pl
from jax.experimental.pallas import tpu as pltpu

F32 = jnp.float32
BF16 = jnp.bfloat16
I32 = jnp.int32

PAGE_SIZE = 128
HEAD_DIM = 128
NSA_HEADS = 16
CMP_BLOCK = 32
SEL_BLOCK = 64
N_SELECT = 16
WINDOW = 512
MLA_HEADS = 16
MLA_Q_LORA = 896
MLA_KV_LORA = 320
MLA_NOPE = 128
MLA_ROPE = 64
MLA_V = 128
MLA_ROW = MLA_KV_LORA + MLA_ROPE
N_GROUPS = 8
EXPERTS_PER_GROUP = 8
N_EXPERTS = N_GROUPS * EXPERTS_PER_GROUP
TOPK_IN_GROUP = 2
ROPE_THETA = 10000.0
RMS_EPS = 1e-6
QUERY_BLOCK = 128
FORCE_BONUS = 1e4
NEG_INF = -1e30
TINY = 1e-30
NSA_SCALE = HEAD_DIM ** -0.5
MLA_SCALE = (MLA_NOPE + MLA_ROPE) ** -0.5
NSA_Q_COLS = NSA_HEADS * HEAD_DIM
NSA_KV_COLS = 3 * 2 * HEAD_DIM
NSA_GATE_COLS = 3 * NSA_HEADS
GROUP_WIDTH = NSA_HEADS * HEAD_DIM

LANES = 128
VMEM_LIMIT = 56 * 1024 * 1024

ZC_Q = 0
ZC_KV = ZC_Q + NSA_Q_COLS
ZC_CQ = ZC_KV + NSA_KV_COLS
ZC_MLA = ZC_CQ + MLA_Q_LORA
ZC_GATE = ZC_MLA + MLA_ROW
Z_COLS = ZC_GATE + LANES

_CONTRACT_LAST = (((1,), (1,)), ((), ()))


def _cparams(sem, vmem=VMEM_LIMIT):
    return pltpu.CompilerParams(dimension_semantics=sem, vmem_limit_bytes=vmem)


def _rms(x, g):
    return x * lax.rsqrt(jnp.mean(x * x, axis=-1, keepdims=True) + RMS_EPS) * g


def _softmax_heads(s3, bias, maskf):
    s = s3 + bias[None]
    m = jnp.max(s, axis=-1, keepdims=True)
    e = jnp.exp(s - m) * maskf[None]
    return e / jnp.maximum(jnp.sum(e, axis=-1, keepdims=True), TINY)


def _norm_matmul_kernel(x_ref, g_ref, w_ref, o_ref, xn_ref):
    @pl.when(pl.program_id(1) == 0)
    def _():
        xn_ref[...] = _rms(x_ref[...], g_ref[...]).astype(BF16)

    o_ref[...] = jnp.dot(xn_ref[...], w_ref[...], preferred_element_type=F32)


def norm_matmul(x, g, w, tm, tn):
    n, d = x.shape
    c = w.shape[1]
    return pl.pallas_call(
        _norm_matmul_kernel,
        out_shape=jax.ShapeDtypeStruct((n, c), F32),
        grid=(n // tm, c // tn),
        in_specs=[pl.BlockSpec((tm, d), lambda i, j: (i, 0)),
                  pl.BlockSpec((1, d), lambda i, j: (0, 0)),
                  pl.BlockSpec((d, tn), lambda i, j: (0, j))],
        out_specs=pl.BlockSpec((tm, tn), lambda i, j: (i, j)),
        scratch_shapes=[pltpu.VMEM((tm, d), BF16)],
        compiler_params=_cparams(("parallel", "arbitrary")),
        name="norm_in_proj",
    )(x, g.reshape(1, d), w)


def _rope_half(x, cos, sin):
    return x * cos + pltpu.roll(x, 64, 1) * sin


def _rope_quarter(x, cos, sin):
    lane = lax.broadcasted_iota(I32, x.shape, 1)
    partner = jnp.where((lane % 64) < 32, pltpu.roll(x, 96, 1), pltpu.roll(x, 32, 1))
    return x * cos + partner * sin


def _split_kernel(z_ref, c128_ref, s128_ref, c64_ref, s64_ref, gq_ref, gkv_ref, wuq_ref, wuk_ref,
                  q_ref, kvc_ref, kvs_ref, kvw_ref, kvsb_ref, kvwb_ref, gate_ref, qcat_ref, row_ref, rowb_ref):
    c128, s128 = c128_ref[...], s128_ref[...]
    c64, s64 = c64_ref[...], s64_ref[...]
    for h in range(NSA_HEADS):
        x = z_ref[:, ZC_Q + h * 128:ZC_Q + (h + 1) * 128]
        q_ref[:, h * 128:(h + 1) * 128] = (_rope_half(x, c128, s128) * NSA_SCALE).astype(BF16)
    for br, (o32, o16) in enumerate(((kvc_ref, None), (kvs_ref, kvsb_ref), (kvw_ref, kvwb_ref))):
        c0 = ZC_KV + br * 256
        k = _rope_half(z_ref[:, c0:c0 + 128], c128, s128)
        v = z_ref[:, c0 + 128:c0 + 256]
        o32[pl.ds(0, k.shape[0], stride=2), :] = k
        o32[pl.ds(1, k.shape[0], stride=2), :] = v
        if o16 is not None:
            o16[:, 0:128] = k.astype(BF16)
            o16[:, 128:256] = v.astype(BF16)
    gate_ref[...] = jax.nn.sigmoid(z_ref[:, ZC_GATE:ZC_GATE + LANES])

    zc = z_ref[:, ZC_MLA:ZC_MLA + MLA_ROW]
    lane384 = lax.broadcasted_iota(I32, zc.shape, 1)
    sq = jnp.where(lane384 < MLA_KV_LORA, zc * zc, 0.0)
    rstd = lax.rsqrt(jnp.sum(sq, axis=-1, keepdims=True) / MLA_KV_LORA + RMS_EPS)
    normed = zc * rstd * gkv_ref[...]
    x3 = zc[:, 256:384]
    lane128 = lax.broadcasted_iota(I32, x3.shape, 1)
    col3 = jnp.where(lane128 < 64, normed[:, 256:384], _rope_quarter(x3, c64, s64))
    row_ref[:, 0:256] = normed[:, 0:256]
    row_ref[:, 256:384] = col3
    rowb_ref[:, 0:256] = normed[:, 0:256].astype(BF16)
    rowb_ref[:, 256:384] = col3.astype(BF16)

    cq = _rms(z_ref[:, ZC_CQ:ZC_CQ + MLA_Q_LORA], gq_ref[...]).astype(BF16)
    qm = jnp.dot(cq, wuq_ref[...], preferred_element_type=F32)
    pe_base = MLA_HEADS * MLA_NOPE
    for hp in range(MLA_HEADS // 2):
        pe = _rope_quarter(qm[:, pe_base + hp * 128:pe_base + (hp + 1) * 128], c64, s64)
        for sub in range(2):
            h = 2 * hp + sub
            nope = qm[:, h * 128:(h + 1) * 128].astype(BF16)
            ql = jnp.dot(nope, wuk_ref[h], preferred_element_type=F32)
            pe_hi = pe if sub == 1 else pltpu.roll(pe, 64, 1)
            tail = ql[:, 256:384] + jnp.where(lane128 >= 64, pe_hi, 0.0)
            qcat_ref[:, h * 384:h * 384 + 256] = (ql[:, 0:256] * MLA_SCALE).astype(BF16)
            qcat_ref[:, h * 384 + 256:(h + 1) * 384] = (tail * MLA_SCALE).astype(BF16)


def split_projection(z, tabs, tab_index, gq, gkv, wuq, wuk, tm):
    n = z.shape[0]
    c128, s128, c64, s64 = tabs
    row = lambda w: pl.BlockSpec((tm, w), lambda i: (i, 0))
    tab = pl.BlockSpec((tm, LANES), lambda i: (tab_index(i), 0))
    full = lambda a: pl.BlockSpec(a.shape, lambda i: (0,) * a.ndim)
    outs = [((n, NSA_Q_COLS), BF16), ((2 * n, 128), F32), ((2 * n, 128), F32), ((2 * n, 128), F32), ((n, 256), BF16),
            ((n, 256), BF16), ((n, LANES), F32), ((n, MLA_HEADS * MLA_ROW), BF16), ((n, MLA_ROW), F32),
            ((n, MLA_ROW), BF16)]
    return pl.pallas_call(
        _split_kernel,
        out_shape=[jax.ShapeDtypeStruct(s, d) for s, d in outs],
        grid=(n // tm,),
        in_specs=[row(Z_COLS), tab, tab, tab, tab, full(gq), full(gkv), full(wuq), full(wuk)],
        out_specs=[pl.BlockSpec((tm * s[0] // n, s[1]), lambda i: (i, 0)) for s, _ in outs],
        compiler_params=_cparams(("parallel",)),
        name="split_projection",
    )(z, c128, s128, c64, s64, gq, gkv, wuq, wuk)


ROWS_PER_CMP_BLOCK = 2 * CMP_BLOCK


def _compress_blocks(load_rows, pos_ref, w_ref, kv):
    pieces = [(load_rows(2 * l + kv) + pos_ref[2 * l + kv:2 * l + kv + 1, :]).astype(BF16) for l in range(CMP_BLOCK)]
    return jnp.dot(jnp.concatenate(pieces, axis=1), w_ref[kv], preferred_element_type=F32)


def _compress_kernel(x_ref, pos_ref, w_ref, o_ref):
    nblk = o_ref.shape[0]
    for kv in range(2):
        o_ref[:, kv * 128:(kv + 1) * 128] = _compress_blocks(
            lambda r: x_ref[pl.ds(r, nblk, stride=ROWS_PER_CMP_BLOCK), :], pos_ref, w_ref, kv)


def compress_rows(x, pos, w, nblk):
    rows = x.shape[0]
    step = nblk * ROWS_PER_CMP_BLOCK
    return pl.pallas_call(
        _compress_kernel,
        out_shape=jax.ShapeDtypeStruct((rows // ROWS_PER_CMP_BLOCK, 256), F32),
        grid=(rows // step,),
        in_specs=[pl.BlockSpec((step, 128), lambda i: (i, 0)),
                  pl.BlockSpec(pos.shape, lambda i: (0, 0)),
                  pl.BlockSpec(w.shape, lambda i: (0, 0, 0))],
        out_specs=pl.BlockSpec((nblk, 256), lambda i: (i, 0)),
        compiler_params=_cparams(("parallel",)),
        name="compress_prompt",
    )(x, pos, w)


def _nsa_prompt_kernel(q_ref, g_ref, ck_ref, ks_ref, kw_ref, o_ref, *, seq, kc):
    H, QB = NSA_HEADS, QUERY_BLOCK
    n_sel = seq // SEL_BLOCK
    half = ck_ref.shape[1] // 2
    qb = pl.program_id(1)
    start = qb * QB
    Q = jnp.concatenate([q_ref[:, h * 128:(h + 1) * 128] for h in range(H)], axis=0)
    qpos = start + lax.broadcasted_iota(I32, (QB, 1), 0)

    ck = ck_ref[0]
    s_c = lax.dot_general(Q, ck[:, :128].astype(BF16), _CONTRACT_LAST, preferred_element_type=F32)
    lane = lax.broadcasted_iota(I32, (1, 2 * half), 1)
    jj = lane % half
    blk = 2 * jj + (lane >= half).astype(I32)
    mask_c = (((blk + 1) * CMP_BLOCK - 1) <= qpos) & (jj < n_sel)
    p_c = _softmax_heads(s_c.reshape(H, QB, 2 * half), jnp.where(mask_c, 0.0, NEG_INF), mask_c.astype(F32))
    o_c = jnp.dot(p_c.reshape(H * QB, 2 * half).astype(BF16), ck[:, 128:].astype(BF16),
                  preferred_element_type=F32)

    imp = jnp.sum(p_c, axis=0)
    imp = imp[:, :half] + imp[:, half:]
    j = lax.broadcasted_iota(I32, (1, half), 1)
    cur = qpos // SEL_BLOCK
    valid = (j * SEL_BLOCK <= qpos) & (j < n_sel)
    forced = (j == 0) | (j == cur) | (j == cur - 1)
    score = jnp.where(valid, imp + jnp.where(forced, FORCE_BONUS, 0.0), -jnp.inf)
    rank = jnp.zeros((QB, half), F32)
    for i in range(n_sel):
        col = score[:, i:i + 1]
        ahead = (col > score) | ((col == score) & (j > i))
        rank = rank + ahead.astype(F32)
    chosen = ((rank < min(N_SELECT, n_sel)) & valid).astype(BF16)
    jcol = lax.broadcasted_iota(I32, (half, 1), 0)

    def sel_body(c, carry):
        m, l, acc = carry
        k0 = pl.multiple_of(c * kc, kc)
        kk = ks_ref[pl.ds(k0, kc), 0:128]
        vv = ks_ref[pl.ds(k0, kc), 128:256]
        s = lax.dot_general(Q, kk, _CONTRACT_LAST, preferred_element_type=F32).reshape(H, QB, kc)
        kidx = k0 + lax.broadcasted_iota(I32, (1, kc), 1)
        expand = ((kidx // SEL_BLOCK) == jcol).astype(BF16)
        picked = jnp.dot(chosen, expand, preferred_element_type=F32) > 0.5
        ok = picked & (kidx <= qpos)
        s = s + jnp.where(ok, 0.0, NEG_INF)[None]
        m_new = jnp.maximum(m, jnp.max(s, axis=-1, keepdims=True))
        alpha = jnp.exp(m - m_new)
        e = jnp.exp(s - m_new)
        l = alpha * l + jnp.sum(e, axis=-1, keepdims=True)
        pv = jnp.dot(e.reshape(H * QB, kc).astype(BF16), vv, preferred_element_type=F32)
        acc = alpha.reshape(H * QB, 1) * acc + pv
        return m_new, l, acc

    n_chunks = (start + QB + kc - 1) // kc
    m0 = jnp.full((H, QB, 1), NEG_INF, F32)
    l0 = jnp.zeros((H, QB, 1), F32)
    a0 = jnp.zeros((H * QB, 128), F32)
    _, l_s, acc_s = lax.fori_loop(0, n_chunks, sel_body, (m0, l0, a0))
    o_s = acc_s / jnp.maximum(l_s, TINY).reshape(H * QB, 1)

    wk = WINDOW + QB
    w0 = pl.multiple_of(jnp.clip(start - WINDOW, 0, seq - wk), QB)
    kk = kw_ref[pl.ds(w0, wk), 0:128]
    vv = kw_ref[pl.ds(w0, wk), 128:256]
    s_w = lax.dot_general(Q, kk, _CONTRACT_LAST, preferred_element_type=F32).reshape(H, QB, wk)
    kidx = w0 + lax.broadcasted_iota(I32, (1, wk), 1)
    mask_w = (kidx <= qpos) & (kidx > qpos - WINDOW)
    p_w = _softmax_heads(s_w, jnp.where(mask_w, 0.0, NEG_INF), mask_w.astype(F32))
    o_w = jnp.dot(p_w.reshape(H * QB, wk).astype(BF16), vv, preferred_element_type=F32)

    g = g_ref[...]
    for h in range(H):
        r = slice(h * QB, (h + 1) * QB)
        o_ref[:, h * 128:(h + 1) * 128] = (g[:, h:h + 1] * o_c[r] + g[:, H + h:H + h + 1] * o_s[r]
                                           + g[:, 2 * H + h:2 * H + h + 1] * o_w[r])


def nsa_prompt(q, gates, ck, kvs_b, kvw_b, batch, seq, kc=512):
    nqb = seq // QUERY_BLOCK
    assert seq % kc == 0 and seq >= WINDOW + QUERY_BLOCK
    return pl.pallas_call(
        functools.partial(_nsa_prompt_kernel, seq=seq, kc=kc),
        out_shape=jax.ShapeDtypeStruct((batch * seq, GROUP_WIDTH), F32),
        grid=(batch, nqb),
        in_specs=[pl.BlockSpec((QUERY_BLOCK, GROUP_WIDTH), lambda b, i: (b * nqb + i, 0)),
                  pl.BlockSpec((QUERY_BLOCK, LANES), lambda b, i: (b * nqb + i, 0)),
                  pl.BlockSpec((1,) + ck.shape[1:], lambda b, i: (b, 0, 0)),
                  pl.BlockSpec((seq, 256), lambda b, i: (b, 0)),
                  pl.BlockSpec((seq, 256), lambda b, i: (b, 0))],
        out_specs=pl.BlockSpec((QUERY_BLOCK, GROUP_WIDTH), lambda b, i: (b * nqb + i, 0)),
        compiler_params=_cparams(("parallel", "parallel")),
        name="nsa_prompt",
    )(q, gates, ck, kvs_b, kvw_b)


def _mla_prompt_kernel(q_ref, rows_ref, o_ref, *, kc):
    H, QB = MLA_HEADS, QUERY_BLOCK
    start = pl.program_id(1) * QB
    Q = jnp.concatenate([q_ref[:, h * MLA_ROW:(h + 1) * MLA_ROW] for h in range(H)], axis=0)
    qpos = start + lax.broadcasted_iota(I32, (QB, 1), 0)

    def body(c, carry, causal):
        m, l, acc = carry
        k0 = pl.multiple_of(c * kc, kc)
        rows = rows_ref[pl.ds(k0, kc), :]
        s = lax.dot_general(Q, rows, _CONTRACT_LAST, preferred_element_type=F32).reshape(H, QB, kc)
        if causal:
            ok = (k0 + lax.broadcasted_iota(I32, (1, kc), 1)) <= qpos
            s = s + jnp.where(ok, 0.0, NEG_INF)[None]
        m_new = jnp.maximum(m, jnp.max(s, axis=-1, keepdims=True))
        alpha = jnp.exp(m - m_new)
        e = jnp.exp(s - m_new)
        l = alpha * l + jnp.sum(e, axis=-1, keepdims=True)
        pv = jnp.dot(e.reshape(H * QB, kc).astype(BF16), rows, preferred_element_type=F32)
        acc = alpha.reshape(H * QB, 1) * acc + pv
        return m_new, l, acc

    n_full = (start + 1) // kc
    n_chunks = (start + QB + kc - 1) // kc
    m0 = jnp.full((H, QB, 1), NEG_INF, F32)
    l0 = jnp.zeros((H, QB, 1), F32)
    a0 = jnp.zeros((H * QB, MLA_ROW), F32)
    carry = lax.fori_loop(0, n_full, functools.partial(body, causal=False), (m0, l0, a0))
    _, l, acc = lax.fori_loop(n_full, n_chunks, functools.partial(body, causal=True), carry)
    o = (acc / jnp.maximum(l, TINY).reshape(H * QB, 1)).astype(BF16)
    for h in range(H):
        o_ref[:, h * MLA_ROW:(h + 1) * MLA_ROW] = o[h * QB:(h + 1) * QB]


def mla_prompt(qcat, rows_b, batch, seq, kc=512):
    nqb = seq // QUERY_BLOCK
    width = MLA_HEADS * MLA_ROW
    return pl.pallas_call(
        functools.partial(_mla_prompt_kernel, kc=kc),
        out_shape=jax.ShapeDtypeStruct((batch * seq, width), BF16),
        grid=(batch, nqb),
        in_specs=[pl.BlockSpec((QUERY_BLOCK, width), lambda b, i: (b * nqb + i, 0)),
                  pl.BlockSpec((seq, MLA_ROW), lambda b, i: (b, 0))],
        out_specs=pl.BlockSpec((QUERY_BLOCK, width), lambda b, i: (b * nqb + i, 0)),
        compiler_params=_cparams(("parallel", "parallel")),
        name="mla_prompt",
    )(qcat, rows_b)


def _gather_pipeline(pt_ref, page_copies, ch):
    b, c = pl.program_id(0), pl.program_id(1)
    nc = pl.num_programs(1)
    t = b * nc + c
    total = pl.num_programs(0) * nc
    slot = t % 2

    def start(tb, tc, sl):
        for p in range(ch):
            for cp in page_copies(pt_ref[tb, tc * ch + p], p, sl):
                cp.start()

    @pl.when(t == 0)
    def _():
        start(0, 0, 0)

    @pl.when(t + 1 < total)
    def _():
        start((t + 1) // nc, (t + 1) % nc, 1 - slot)

    for p in range(ch):
        for cp in page_copies(0, p, slot):
            cp.wait()
    return slot


def _cmp_sample_kernel(pt_ref, q_ref, w_ref, pos_ref, cache_ref, oc_ref, imp_ref,
                       buf, sem, ck_scr, *, ch, past, dec_seq):
    c = pl.program_id(1)
    blocks_per_page = PAGE_SIZE // CMP_BLOCK
    nsb = ch * blocks_per_page // 2

    def page_copies(page, p, sl):
        copies = []
        for blk in range(blocks_per_page):
            col = (blk % 2) * nsb + p * (blocks_per_page // 2) + blk // 2
            row0 = pl.multiple_of((page * blocks_per_page + blk) * ROWS_PER_CMP_BLOCK, ROWS_PER_CMP_BLOCK)
            copies.append(pltpu.make_async_copy(cache_ref.at[pl.ds(row0, ROWS_PER_CMP_BLOCK)],
                                                buf.at[sl, :, pl.ds(col, 1), :], sem.at[sl]))
        return copies

    slot = _gather_pipeline(pt_ref, page_copies, ch)
    r0 = pl.multiple_of(c * nsb, nsb)
    for kv in range(2):
        ckp = _compress_blocks(lambda r: buf[slot, r], pos_ref, w_ref, kv)
        ck_scr[0, kv, pl.ds(r0, nsb), :] = ckp[:nsb]
        ck_scr[1, kv, pl.ds(r0, nsb), :] = ckp[nsb:]

    @pl.when(c == pl.num_programs(1) - 1)
    def _():
        H = NSA_HEADS
        nsel = ck_scr.shape[2]
        ckk = jnp.concatenate([ck_scr[0, 0], ck_scr[1, 0]], axis=0).astype(BF16)
        ckv = jnp.concatenate([ck_scr[0, 1], ck_scr[1, 1]], axis=0).astype(BF16)
        q = q_ref[0]
        s = lax.dot_general(q, ckk, _CONTRACT_LAST, preferred_element_type=F32)
        lane = lax.broadcasted_iota(I32, (1, 2 * nsel), 1)
        blk = 2 * (lane % nsel) + (lane >= nsel).astype(I32)
        qpos = past + lax.broadcasted_iota(I32, (dec_seq * H, 1), 0) // H
        mask = ((blk + 1) * CMP_BLOCK - 1) <= qpos
        s = jnp.where(mask, s, NEG_INF)
        m = jnp.max(s, axis=-1, keepdims=True)
        e = jnp.where(mask, jnp.exp(s - m), 0.0)
        p = e / jnp.maximum(jnp.sum(e, axis=-1, keepdims=True), TINY)
        oc_ref[0] = jnp.dot(p.astype(BF16), ckv, preferred_element_type=F32)
        imp = jnp.sum(p.reshape(dec_seq, H, 2 * nsel), axis=1)
        imp_ref[0] = imp[:, :nsel] + imp[:, nsel:]


def cmp_sample(page_table, q_s, w_cmp, pos_cmp, cache_rows, dec_seq, ch):
    db, n_pages = page_table.shape
    nsel = 2 * n_pages
    rows = dec_seq * NSA_HEADS
    grid_spec = pltpu.PrefetchScalarGridSpec(
        num_scalar_prefetch=1,
        grid=(db, n_pages // ch),
        in_specs=[pl.BlockSpec((1, rows, 128), lambda b, c, pt: (b, 0, 0)),
                  pl.BlockSpec(w_cmp.shape, lambda b, c, pt: (0, 0, 0)),
                  pl.BlockSpec(pos_cmp.shape, lambda b, c, pt: (0, 0)),
                  pl.BlockSpec(memory_space=pl.ANY)],
        out_specs=[pl.BlockSpec((1, rows, 128), lambda b, c, pt: (b, 0, 0)),
                   pl.BlockSpec((1, dec_seq, nsel), lambda b, c, pt: (b, 0, 0))],
        scratch_shapes=[pltpu.VMEM((2, ROWS_PER_CMP_BLOCK, ch * PAGE_SIZE // CMP_BLOCK, 128), F32),
                        pltpu.SemaphoreType.DMA((2,)),
                        pltpu.VMEM((2, 2, nsel, 128), F32)])
    return pl.pallas_call(
        functools.partial(_cmp_sample_kernel, ch=ch, past=n_pages * PAGE_SIZE, dec_seq=dec_seq),
        out_shape=[jax.ShapeDtypeStruct((db, rows, 128), F32), jax.ShapeDtypeStruct((db, dec_seq, nsel), F32)],
        grid_spec=grid_spec,
        compiler_params=_cparams(("arbitrary", "arbitrary")),
        name="cmp_sample",
    )(page_table, q_s, w_cmp, pos_cmp, cache_rows)


def _select_kernel(imp_ref, idx_ref, val_ref, *, past, dec_seq, n_sel):
    nq, n_past = imp_ref.shape
    x = jnp.concatenate([imp_ref[...], jnp.zeros((nq, LANES), F32)], axis=1)
    width = n_past + LANES
    j = lax.broadcasted_iota(I32, (1, width), 1)
    jf = j.astype(F32)
    qpos = past + lax.broadcasted_iota(I32, (nq, 1), 0) % dec_seq
    cur = qpos // SEL_BLOCK
    valid = (j * SEL_BLOCK <= qpos) & (j < n_sel)
    forced = (j == 0) | (j == cur) | (j == cur - 1)
    score = jnp.where(valid, x + jnp.where(forced, FORCE_BONUS, 0.0), -jnp.inf)
    out_lane = lax.broadcasted_iota(I32, (1, LANES), 1)
    idx = jnp.zeros((nq, LANES), F32)
    val = jnp.zeros((nq, LANES), F32)
    for r in range(min(N_SELECT, n_sel)):
        m = jnp.max(score, axis=-1, keepdims=True)
        ix = jnp.min(jnp.where(score == m, jf, float(width)), axis=-1, keepdims=True)
        idx = jnp.where(out_lane == r, ix, idx)
        val = jnp.where(out_lane == r, (m > -jnp.inf).astype(F32), val)
        score = jnp.where(jf == ix, -jnp.inf, score)
    idx_ref[...] = idx.astype(I32)
    val_ref[...] = val.astype(I32)


def select_sample(imp, past, dec_seq, n_sel):
    nq = imp.shape[0]
    return pl.pallas_call(
        functools.partial(_select_kernel, past=past, dec_seq=dec_seq, n_sel=n_sel),
        out_shape=[jax.ShapeDtypeStruct((nq, LANES), I32), jax.ShapeDtypeStruct((nq, LANES), I32)],
        compiler_params=_cparams(None),
        name="select_sample",
    )(imp)


ROWS_PER_SEL_BLOCK = 2 * SEL_BLOCK


def _sel_copy(cache_ref, new_ref, pt_ref, idx_ref, b, e, buf, slot, sem, n_pages):
    n_past_blk = 2 * n_pages
    ix = idx_ref[b, e]
    dst = buf.at[slot, pl.ds(e * ROWS_PER_SEL_BLOCK, ROWS_PER_SEL_BLOCK), :]
    is_new = ix >= n_past_blk
    page = pt_ref[b, jnp.minimum(ix // 2, n_pages - 1)]
    row0 = pl.multiple_of((2 * page + ix % 2) * ROWS_PER_SEL_BLOCK, ROWS_PER_SEL_BLOCK)

    def new_copy():
        return pltpu.make_async_copy(new_ref.at[b], dst, sem.at[slot])

    def past_copy():
        return pltpu.make_async_copy(cache_ref.at[pl.ds(row0, ROWS_PER_SEL_BLOCK), :], dst, sem.at[slot])

    return is_new, new_copy, past_copy


def _masked_softmax(s, ok):
    s = jnp.where(ok, s, NEG_INF)
    m = jnp.max(s, axis=-1, keepdims=True)
    e = jnp.where(ok, jnp.exp(s - m), 0.0)
    return e / jnp.maximum(jnp.sum(e, axis=-1, keepdims=True), TINY)


def _sel_sample_kernel(pt_ref, idx_ref, q_ref, ok_ref, oc_ref, g_ref, win_ref, wnew_ref, wmask_ref,
                       cache_ref, new_ref, o_ref, buf, sem, kw_scr, *, dec_seq, n_pages):
    H = NSA_HEADS
    nsl = dec_seq * N_SELECT
    b = pl.program_id(0)
    nb = pl.num_programs(0)
    slot = b % 2

    def start(tb, sl):
        for e in range(nsl):
            is_new, new_copy, past_copy = _sel_copy(cache_ref, new_ref, pt_ref, idx_ref, tb, e, buf, sl, sem, n_pages)

            @pl.when(is_new)
            def _():
                new_copy().start()

            @pl.when(jnp.logical_not(is_new))
            def _():
                past_copy().start()

    @pl.when(b == 0)
    def _():
        start(0, 0)

    @pl.when(b + 1 < nb)
    def _():
        start(b + 1, 1 - slot)

    for e in range(nsl):
        pltpu.make_async_copy(new_ref.at[0], buf.at[slot, pl.ds(e * ROWS_PER_SEL_BLOCK, ROWS_PER_SEL_BLOCK), :],
                              sem.at[slot]).wait()

    q = q_ref[0]
    nk = N_SELECT * SEL_BLOCK
    keys = buf[slot, pl.ds(0, nsl * SEL_BLOCK, stride=2), :].astype(BF16).reshape(dec_seq, nk, 128)
    vals = buf[slot, pl.ds(1, nsl * SEL_BLOCK, stride=2), :].astype(BF16).reshape(dec_seq, nk, 128)
    q3 = q.reshape(dec_seq, H, 128)
    s = jnp.einsum('shd,skd->shk', q3, keys, preferred_element_type=F32)
    p = _masked_softmax(s, ok_ref[0][:, None, :] > 0.5)
    o_s = jnp.einsum('shk,skd->shd', p.astype(BF16), vals, preferred_element_type=F32).reshape(dec_seq * H, 128)

    wb = win_ref.shape[1] // 2
    nw = wnew_ref.shape[1] // 2
    for kv in range(2):
        kw_scr[kv, 0:wb, :] = win_ref[0, pl.ds(kv, wb, stride=2), :]
        kw_scr[kv, wb:, :] = jnp.zeros((kw_scr.shape[1] - wb, 128), F32)
        kw_scr[kv, wb:wb + nw, :] = wnew_ref[0, pl.ds(kv, nw, stride=2), :]
    s = lax.dot_general(q, kw_scr[0].astype(BF16), _CONTRACT_LAST, preferred_element_type=F32)
    p = _masked_softmax(s, wmask_ref[...] > 0.5)
    o_w = jnp.dot(p.astype(BF16), kw_scr[1].astype(BF16), preferred_element_type=F32)

    g = g_ref[0]
    o_ref[0] = g[:, 0:1] * oc_ref[0] + g[:, 1:2] * o_s + g[:, 2:3] * o_w


def sel_sample(page_table, idx, q_s, okmask, o_c, gates_rows, win, wnew, wmask, cache_sel, new_blocks, dec_seq):
    db, n_pages = page_table.shape
    rows = dec_seq * NSA_HEADS
    nkeys = N_SELECT * SEL_BLOCK
    wk = wmask.shape[1]
    b3 = lambda shape: pl.BlockSpec((1,) + shape, lambda b, pt, ix: (b, 0, 0))
    grid_spec = pltpu.PrefetchScalarGridSpec(
        num_scalar_prefetch=2,
        grid=(db,),
        in_specs=[b3((rows, 128)), b3((dec_seq, nkeys)), b3((rows, 128)), b3((rows, 3)),
                  b3(win.shape[1:]), b3(wnew.shape[1:]),
                  pl.BlockSpec(wmask.shape, lambda b, pt, ix: (0, 0)),
                  pl.BlockSpec(memory_space=pl.ANY), pl.BlockSpec(memory_space=pl.ANY)],
        out_specs=b3((rows, 128)),
        scratch_shapes=[pltpu.VMEM((2, dec_seq * N_SELECT * ROWS_PER_SEL_BLOCK, 128), F32),
                        pltpu.SemaphoreType.DMA((2,)),
                        pltpu.VMEM((2, wk, 128), F32)])
    return pl.pallas_call(
        functools.partial(_sel_sample_kernel, dec_seq=dec_seq, n_pages=n_pages),
        out_shape=jax.ShapeDtypeStruct((db, rows, 128), F32),
        grid_spec=grid_spec,
        compiler_params=_cparams(("arbitrary",)),
        name="sel_win_sample",
    )(page_table, idx, q_s, okmask, o_c, gates_rows, win, wnew, wmask, cache_sel, new_blocks)


def _mla_sample_kernel(pt_ref, q_ref, new_ref, cache_ref, o_ref, buf, sem, m_scr, l_scr, acc_scr, new_scr,
                       *, ch, dec_seq):
    H = MLA_HEADS
    c = pl.program_id(1)
    def page_copies(page, p, sl):
        return [pltpu.make_async_copy(cache_ref.at[page], buf.at[sl, pl.ds(p * PAGE_SIZE, PAGE_SIZE), :], sem.at[sl])]

    slot = _gather_pipeline(pt_ref, page_copies, ch)
    q = q_ref[0]

    @pl.when(c == 0)
    def _():
        m_scr[...] = jnp.full(m_scr.shape, NEG_INF, F32)
        l_scr[...] = jnp.zeros(l_scr.shape, F32)
        acc_scr[...] = jnp.zeros(acc_scr.shape, F32)

    def update(rows, ok):
        s = lax.dot_general(q, rows, _CONTRACT_LAST, preferred_element_type=F32)
        if ok is not None:
            s = jnp.where(ok, s, NEG_INF)
        m_old = m_scr[...]
        m_new = jnp.maximum(m_old, jnp.max(s, axis=-1, keepdims=True))
        alpha = jnp.exp(m_old - m_new)
        e = jnp.exp(s - m_new)
        if ok is not None:
            e = jnp.where(ok, e, 0.0)
        l_scr[...] = alpha * l_scr[...] + jnp.sum(e, axis=-1, keepdims=True)
        acc_scr[...] = alpha * acc_scr[...] + jnp.dot(e.astype(BF16), rows, preferred_element_type=F32)
        m_scr[...] = m_new

    update(buf[slot].astype(BF16), None)

    @pl.when(c == pl.num_programs(1) - 1)
    def _():
        nr = new_ref.shape[1]
        new_scr[...] = jnp.zeros(new_scr.shape, BF16)
        new_scr[0:nr, :] = new_ref[0]
        t = lax.broadcasted_iota(I32, (1, new_scr.shape[0]), 1)
        srow = lax.broadcasted_iota(I32, (dec_seq * H, 1), 0) // H
        update(new_scr[...], (t <= srow) & (t < dec_seq))
        o_ref[0] = (acc_scr[...] / jnp.maximum(l_scr[...], TINY)).astype(BF16)


def mla_sample(page_table, qcat_s, new_rows, cache, dec_seq, ch):
    db, n_pages = page_table.shape
    rows = dec_seq * MLA_HEADS
    grid_spec = pltpu.PrefetchScalarGridSpec(
        num_scalar_prefetch=1,
        grid=(db, n_pages // ch),
        in_specs=[pl.BlockSpec((1, rows, MLA_ROW), lambda b, c, pt: (b, 0, 0)),
                  pl.BlockSpec((1,) + new_rows.shape[1:], lambda b, c, pt: (b, 0, 0)),
                  pl.BlockSpec(memory_space=pl.ANY)],
        out_specs=pl.BlockSpec((1, rows, MLA_ROW), lambda b, c, pt: (b, 0, 0)),
        scratch_shapes=[pltpu.VMEM((2, ch * PAGE_SIZE, MLA_ROW), F32),
                        pltpu.SemaphoreType.DMA((2,)),
                        pltpu.VMEM((rows, 1), F32), pltpu.VMEM((rows, 1), F32),
                        pltpu.VMEM((rows, MLA_ROW), F32), pltpu.VMEM((LANES, MLA_ROW), BF16)])
    return pl.pallas_call(
        functools.partial(_mla_sample_kernel, ch=ch, dec_seq=dec_seq),
        out_shape=jax.ShapeDtypeStruct((db, rows, MLA_ROW), BF16),
        grid_spec=grid_spec,
        compiler_params=_cparams(("arbitrary", "arbitrary")),
        name="mla_sample",
    )(page_table, qcat_s, new_rows, cache)


def _head_proj_kernel(x_ref, w_ref, o_ref):
    for h in range(MLA_HEADS):
        o_ref[:, h * MLA_V:(h + 1) * MLA_V] = jnp.dot(x_ref[:, h * MLA_ROW:(h + 1) * MLA_ROW], w_ref[h],
                                                      preferred_element_type=F32)


def head_proj(x, w, tm):
    n = x.shape[0]
    return pl.pallas_call(
        _head_proj_kernel,
        out_shape=jax.ShapeDtypeStruct((n, GROUP_WIDTH), F32),
        grid=(n // tm,),
        in_specs=[pl.BlockSpec((tm, x.shape[1]), lambda i: (i, 0)),
                  pl.BlockSpec(w.shape, lambda i: (0, 0, 0))],
        out_specs=pl.BlockSpec((tm, GROUP_WIDTH), lambda i: (i, 0)),
        compiler_params=_cparams(("parallel",)),
        name="mla_value_proj",
    )(x, w)


def _out_proj_kernel(a_ref, b_ref, ga_ref, gb_ref, w_ref, h_ref, o_ref, cat_ref):
    @pl.when(pl.program_id(1) == 0)
    def _():
        cat_ref[:, 0:GROUP_WIDTH] = _rms(a_ref[...], ga_ref[...]).astype(BF16)
        cat_ref[:, GROUP_WIDTH:2 * GROUP_WIDTH] = _rms(b_ref[...], gb_ref[...]).astype(BF16)

    o_ref[...] = h_ref[...] + jnp.dot(cat_ref[...], w_ref[...], preferred_element_type=F32)


def out_proj(o_nsa, o_mla, ga, gb, w, h, tm, tn):
    n, d = h.shape
    gw = GROUP_WIDTH
    return pl.pallas_call(
        _out_proj_kernel,
        out_shape=jax.ShapeDtypeStruct((n, d), F32),
        grid=(n // tm, d // tn),
        in_specs=[pl.BlockSpec((tm, gw), lambda i, j: (i, 0)),
                  pl.BlockSpec((tm, gw), lambda i, j: (i, 0)),
                  pl.BlockSpec((1, gw), lambda i, j: (0, 0)),
                  pl.BlockSpec((1, gw), lambda i, j: (0, 0)),
                  pl.BlockSpec((2 * gw, tn), lambda i, j: (0, j)),
                  pl.BlockSpec((tm, tn), lambda i, j: (i, j))],
        out_specs=pl.BlockSpec((tm, tn), lambda i, j: (i, j)),
        scratch_shapes=[pltpu.VMEM((tm, 2 * gw), BF16)],
        compiler_params=_cparams(("parallel", "arbitrary")),
        name="out_proj",
    )(o_nsa, o_mla, ga, gb, w, h)


def _router_kernel(h_ref, g_ref, w_ref, b_ref, xn_ref, ids_ref, gts_ref):
    xn = _rms(h_ref[...], g_ref[...])
    xn_ref[...] = xn
    logits = jnp.dot(xn, w_ref[...], preferred_element_type=F32, precision=lax.Precision.HIGHEST) + b_ref[...]
    lane = lax.broadcasted_iota(I32, logits.shape, 1)
    lanef = lane.astype(F32)
    big = float(LANES)

    def first_argmax(v):
        m = jnp.max(v, axis=-1, keepdims=True)
        return m, jnp.min(jnp.where(v == m, lanef, big), axis=-1, keepdims=True)

    gl = jnp.where(lane < N_GROUPS, logits, -jnp.inf)
    gmax, gidx = first_argmax(gl)
    g_gate = 1.0 / jnp.sum(jnp.exp(gl - gmax), axis=-1, keepdims=True)
    group_of_lane = ((lane - N_GROUPS) // EXPERTS_PER_GROUP).astype(F32)
    in_group = (lane >= N_GROUPS) & (lane < N_GROUPS + N_EXPERTS) & (group_of_lane == gidx)
    el = jnp.where(in_group, logits, -jnp.inf)
    v0, i0 = first_argmax(el)
    v1, i1 = first_argmax(jnp.where(lanef == i0, -jnp.inf, el))
    e0 = jnp.exp(v0 - v0)
    e1 = jnp.exp(v1 - v0)
    den = e0 + e1
    ids = jnp.where(lane == 0, i0 - N_GROUPS, jnp.where(lane == 1, i1 - N_GROUPS, 0.0))
    gts = jnp.where(lane == 0, g_gate * (e0 / den), jnp.where(lane == 1, g_gate * (e1 / den), 0.0))
    ids_ref[...] = ids.astype(I32)
    gts_ref[...] = gts


def router(h, g, w, b, tm):
    n, d = h.shape
    return pl.pallas_call(
        _router_kernel,
        out_shape=[jax.ShapeDtypeStruct((n, d), F32), jax.ShapeDtypeStruct((n, LANES), I32),
                   jax.ShapeDtypeStruct((n, LANES), F32)],
        grid=(n // tm,),
        in_specs=[pl.BlockSpec((tm, d), lambda i: (i, 0)),
                  pl.BlockSpec((1, d), lambda i: (0, 0)),
                  pl.BlockSpec((d, LANES), lambda i: (0, 0)),
                  pl.BlockSpec((1, LANES), lambda i: (0, 0))],
        out_specs=[pl.BlockSpec((tm, d), lambda i: (i, 0)),
                   pl.BlockSpec((tm, LANES), lambda i: (i, 0)),
                   pl.BlockSpec((tm, LANES), lambda i: (i, 0))],
        compiler_params=_cparams(("parallel",)),
        name="ffn_norm_router",
    )(h, g.reshape(1, d), w, b)


ROW_DMA_WAIT_GROUP = 8
FFN_CAST_CHUNK = 512


def _ffn_kernel(be_ref, rs_ref, cnt_ref, nu_ref, order_ref, x_hbm, wg_ref, wu_ref, wd_ref, y_hbm,
                xf_ref, xb_ref, acc_ref, gsem, ssem, *, n_tok):
    i, f = pl.program_id(0), pl.program_id(1)
    nf = pl.num_programs(1)
    nu = nu_ref[0]
    grp = ROW_DMA_WAIT_GROUP

    def gather_row(blk, r):
        tok = order_ref[rs_ref[blk] + r] // 2
        return pltpu.make_async_copy(x_hbm.at[pl.ds(tok, 1), :], xf_ref.at[pl.ds(r, 1), :], gsem)

    def scatter_row(blk, r):
        entry = order_ref[rs_ref[blk] + r]
        dst = (entry % 2) * n_tok + entry // 2
        return pltpu.make_async_copy(acc_ref.at[blk % 2, pl.ds(r, 1), :], y_hbm.at[pl.ds(dst, 1), :], ssem)

    def split(blk):
        return (cnt_ref[blk] // (2 * grp)) * grp

    def start_rows(blk, row_copy, lo, hi):
        n = hi - lo

        def groups(g, carry):
            for u in range(grp):
                row_copy(blk, lo + g * grp + u).start()
            return carry

        def singles(r, carry):
            row_copy(blk, lo + (n // grp) * grp + r).start()
            return carry
        lax.fori_loop(0, n // grp, groups, 0)
        lax.fori_loop(0, n % grp, singles, 0)

    def wait_rows(blk, row_copy, group_copy):
        n = cnt_ref[blk]

        def groups(g, carry):
            group_copy().wait()
            return carry

        def singles(r, carry):
            row_copy(0, 0).wait()
            return carry
        lax.fori_loop(0, n // grp, groups, 0)
        lax.fori_loop(0, n % grp, singles, 0)

    def gather_group():
        return pltpu.make_async_copy(x_hbm.at[pl.ds(0, grp), :], xf_ref.at[pl.ds(0, grp), :], gsem)

    def scatter_group():
        return pltpu.make_async_copy(acc_ref.at[0, pl.ds(0, grp), :], y_hbm.at[pl.ds(0, grp), :], ssem)

    used = i < nu

    @pl.when(used & (f == 0))
    def _():
        @pl.when(i == 0)
        def _():
            xf_ref[...] = jnp.zeros(xf_ref.shape, F32)
            acc_ref[...] = jnp.zeros(acc_ref.shape, F32)
            start_rows(0, gather_row, 0, cnt_ref[0])

        wait_rows(i, gather_row, gather_group)
        xb_ref[...] = xf_ref[...].astype(BF16)

        @pl.when(i + 1 < nu)
        def _():
            start_rows(i + 1, gather_row, 0, split(i + 1))

        @pl.when(i > 1)
        def _():
            wait_rows(i - 2, scatter_row, scatter_group)

    @pl.when(used & (f == 1))
    def _():
        @pl.when(i + 1 < nu)
        def _():
            start_rows(i + 1, gather_row, split(i + 1), cnt_ref[i + 1])

        @pl.when(i > 0)
        def _():
            start_rows(i - 1, scatter_row, 0, split(i - 1))

    @pl.when(used & (f == 2) & (i > 0))
    def _():
        start_rows(i - 1, scatter_row, split(i - 1), cnt_ref[i - 1])

    @pl.when(used)
    def _():
        acc = acc_ref.at[i % 2]
        d = xb_ref.shape[1]
        gate = up = None
        for k0 in range(0, d, FFN_CAST_CHUNK):
            xk = xb_ref[:, k0:k0 + FFN_CAST_CHUNK]
            g_part = jnp.dot(xk, wg_ref[k0:k0 + FFN_CAST_CHUNK, :].astype(BF16), preferred_element_type=F32)
            u_part = jnp.dot(xk, wu_ref[k0:k0 + FFN_CAST_CHUNK, :].astype(BF16), preferred_element_type=F32)
            gate = g_part if gate is None else gate + g_part
            up = u_part if up is None else up + u_part
        hmid = (jax.nn.silu(gate) * up).astype(BF16)
        first = f == 0
        for n0 in range(0, d, FFN_CAST_CHUNK):
            y = jnp.dot(hmid, wd_ref[:, n0:n0 + FFN_CAST_CHUNK].astype(BF16), preferred_element_type=F32)
            acc[:, n0:n0 + FFN_CAST_CHUNK] = jnp.where(first, y, acc[:, n0:n0 + FFN_CAST_CHUNK] + y)

        @pl.when((f == nf - 1) & (i == nu - 1))
        def _():
            start_rows(i, scatter_row, 0, cnt_ref[i])

            @pl.when(i > 0)
            def _():
                wait_rows(i - 1, scatter_row, scatter_group)
            wait_rows(i, scatter_row, scatter_group)


def expert_ffn(block_expert, row_start, row_count, n_used, order, xn, w_gate, w_up, w_down, tm, tf):
    n_tok, d = xn.shape
    d_exp = w_gate.shape[2]
    nf = d_exp // tf
    assert nf >= 4, "the row DMA schedule uses steps 0, 1, 2 and the last step of a block"
    n_blocks = block_expert.shape[0]

    def blk(i, nu):
        return jnp.minimum(i, nu[0] - 1)

    def fidx(i, f, nu):
        return jnp.where(i < nu[0], f, nf - 1)

    grid_spec = pltpu.PrefetchScalarGridSpec(
        num_scalar_prefetch=5,
        grid=(n_blocks, nf),
        in_specs=[pl.BlockSpec(memory_space=pl.ANY),
                  pl.BlockSpec((None, d, tf), lambda i, f, be, rs, cn, nu, od: (be[blk(i, nu)], 0, fidx(i, f, nu))),
                  pl.BlockSpec((None, d, tf), lambda i, f, be, rs, cn, nu, od: (be[blk(i, nu)], 0, fidx(i, f, nu))),
                  pl.BlockSpec((None, tf, d), lambda i, f, be, rs, cn, nu, od: (be[blk(i, nu)], fidx(i, f, nu), 0))],
        out_specs=pl.BlockSpec(memory_space=pl.ANY),
        scratch_shapes=[pltpu.VMEM((tm, d), F32), pltpu.VMEM((tm, d), BF16), pltpu.VMEM((2, tm, d), F32),
                        pltpu.SemaphoreType.DMA(()), pltpu.SemaphoreType.DMA(())])
    return pl.pallas_call(
        functools.partial(_ffn_kernel, n_tok=n_tok),
        out_shape=jax.ShapeDtypeStruct((TOPK_IN_GROUP * n_tok, d), F32),
        grid_spec=grid_spec,
        compiler_params=_cparams(("arbitrary", "arbitrary")),
        name="expert_ffn",
    )(block_expert, row_start, row_count, n_used, order, xn, w_gate, w_up, w_down)


def _combine_kernel(h_ref, y0_ref, y1_ref, g_ref, o_ref):
    g = g_ref[...]
    o_ref[...] = h_ref[...] + (g[:, 0:1] * y0_ref[...] + g[:, 1:2] * y1_ref[...])


def moe_combine(h, y, gts, tm):
    n, d = h.shape
    nt = n // tm
    return pl.pallas_call(
        _combine_kernel,
        out_shape=jax.ShapeDtypeStruct((n, d), F32),
        grid=(nt,),
        in_specs=[pl.BlockSpec((tm, d), lambda i: (i, 0)),
                  pl.BlockSpec((tm, d), lambda i: (i, 0)),
                  pl.BlockSpec((tm, d), lambda i: (i + nt, 0)),
                  pl.BlockSpec((tm, LANES), lambda i: (i, 0))],
        out_specs=pl.BlockSpec((tm, d), lambda i: (i, 0)),
        compiler_params=_cparams(("parallel",)),
        name="moe_combine",
    )(h, y, y, gts)


def _ple_kernel(h_ref, hc_ref, g_ref, wg_ref, p_ref, wp_ref, o_ref, hn_ref):
    @pl.when(pl.program_id(1) == 0)
    def _():
        hn_ref[...] = _rms(h_ref[...], g_ref[...]).astype(BF16)

    gate = jax.nn.sigmoid(jnp.dot(hn_ref[...], wg_ref[...], preferred_element_type=F32))
    proj = jnp.dot(p_ref[...], wp_ref[...], preferred_element_type=F32)
    o_ref[...] = hc_ref[...] + gate * proj


def ple_add(h, g, wg, p, wp, tm, tn):
    n, d = h.shape
    pd = p.shape[1]
    return pl.pallas_call(
        _ple_kernel,
        out_shape=jax.ShapeDtypeStruct((n, d), F32),
        grid=(n // tm, d // tn),
        in_specs=[pl.BlockSpec((tm, d), lambda i, j: (i, 0)),
                  pl.BlockSpec((tm, tn), lambda i, j: (i, j)),
                  pl.BlockSpec((1, d), lambda i, j: (0, 0)),
                  pl.BlockSpec((d, tn), lambda i, j: (0, j)),
                  pl.BlockSpec((tm, pd), lambda i, j: (i, 0)),
                  pl.BlockSpec((pd, tn), lambda i, j: (0, j))],
        out_specs=pl.BlockSpec((tm, tn), lambda i, j: (i, j)),
        scratch_shapes=[pltpu.VMEM((tm, d), BF16)],
        compiler_params=_cparams(("parallel", "arbitrary")),
        name="ple_add",
    )(h, h, g.reshape(1, d), wg, p, wp)


def _final_norm_kernel(x_ref, g_ref, o_ref):
    o_ref[...] = _rms(x_ref[...], g_ref[...])


def final_norm(x, g, tm):
    n, d = x.shape
    return pl.pallas_call(
        _final_norm_kernel,
        out_shape=jax.ShapeDtypeStruct((n, d), F32),
        grid=(n // tm,),
        in_specs=[pl.BlockSpec((tm, d), lambda i: (i, 0)), pl.BlockSpec((1, d), lambda i: (0, 0))],
        out_specs=pl.BlockSpec((tm, d), lambda i: (i, 0)),
        compiler_params=_cparams(("parallel",)),
        name="final_norm",
    )(x, g.reshape(1, d))


def _rope_tables(pos, tm):
    pos = np.asarray(pos, np.float32)[:, None]

    def table(dim):
        inv = np.float32(ROPE_THETA) ** (-np.arange(0, dim, 2, dtype=np.float32) / np.float32(dim))
        ang = (pos * inv[None, :]).astype(np.float32).astype(np.float64)
        cos, sin = np.cos(ang).astype(np.float32), np.sin(ang).astype(np.float32)
        reps = LANES // dim
        return np.tile(np.concatenate([cos, cos], 1), (1, reps)), np.tile(np.concatenate([-sin, sin], 1), (1, reps))

    c128, s128 = table(HEAD_DIM)
    c64, s64 = table(MLA_ROPE)
    return tuple(jnp.asarray(t) for t in (c128, s128, c64, s64))


def _compress_weights(cmp_pos, cmp_w):
    w = cmp_w.reshape(2, CMP_BLOCK * HEAD_DIM, HEAD_DIM).astype(BF16)
    pos = jnp.transpose(cmp_pos, (1, 0, 2)).reshape(ROWS_PER_CMP_BLOCK, HEAD_DIM)
    return w, pos


def _layer(hp, hs, p_all, cache_cmp, cache_sel, cache_mla, win_state, page_table, wts):
    (norm_attn, w_in, cmp_pos, cmp_w, mla_q_norm, mla_kv_norm, mla_w_uq, mla_w_uk, mla_w_uv, group_norm, w_out,
     norm_ffn, rg_w, rg_b, re_w, re_b, w_gate, w_up, w_down, ple_proj, ple_norm, ple_gate_w) = wts
    batch, seq, d_model = hp.shape
    db, dec_seq, _ = hs.shape
    n_pages = page_table.shape[1]
    past = n_pages * PAGE_SIZE
    n_p, n_s = batch * seq, db * dec_seq
    n = n_p + n_s
    tm = int(np.gcd(np.gcd(512, n_s), seq))
    assert n_p % tm == 0 and n_s % tm == 0 and seq % tm == 0 and tm % dec_seq == 0
    assert dec_seq < CMP_BLOCK and dec_seq <= SEL_BLOCK and past % SEL_BLOCK == 0 and past >= WINDOW
    assert win_state.shape[1] == WINDOW

    h = jnp.concatenate([hp.reshape(n_p, d_model), hs.reshape(n_s, d_model)], axis=0)

    c0, c1 = NSA_Q_COLS, NSA_Q_COLS + NSA_KV_COLS
    c2 = c1 + NSA_GATE_COLS
    w_in_p = jnp.concatenate([w_in[:, :c1], w_in[:, c2:], w_in[:, c1:c2],
                              jnp.zeros((d_model, LANES - NSA_GATE_COLS), w_in.dtype)], axis=1).astype(BF16)
    z = norm_matmul(h, norm_attn, w_in_p, tm, 384)

    tm2 = min(256, tm)
    pos_rows = np.concatenate([np.arange(seq), past + np.arange(tm2) % dec_seq])
    tabs = _rope_tables(pos_rows, tm2)
    n_ptiles, seq_tiles = n_p // tm2, seq // tm2
    tab_index = lambda i: jnp.where(i < n_ptiles, i % seq_tiles, seq_tiles)
    wuq = jnp.concatenate([mla_w_uq[:, :, :MLA_NOPE].reshape(MLA_Q_LORA, -1),
                           mla_w_uq[:, :, MLA_NOPE:].reshape(MLA_Q_LORA, -1)], axis=1).astype(BF16)
    wuk = jnp.pad(jnp.transpose(mla_w_uk, (1, 2, 0)), ((0, 0), (0, 0), (0, MLA_ROPE))).astype(BF16)
    gkv = jnp.pad(mla_kv_norm, (0, MLA_ROPE)).reshape(1, MLA_ROW)
    (q_nsa, kv_c, kv_s, kv_w, kvs_b, kvw_b, gates, qcat, mla_row, mla_row_b) = split_projection(
        z, tabs, tab_index, mla_q_norm.reshape(1, -1), gkv, wuq, wuk, tm2)

    w_cmp, pos_cmp = _compress_weights(cmp_pos, cmp_w)
    nb = seq // CMP_BLOCK
    ck = compress_rows(kv_c[:2 * n_p], pos_cmp, w_cmp, min(256, n_p // CMP_BLOCK))
    half = -(-(nb // 2) // LANES) * LANES
    ck = ck.reshape(batch, nb // 2, 2, 256)
    padh = ((0, 0), (0, half - nb // 2), (0, 0))
    ck = jnp.concatenate([jnp.pad(ck[:, :, 0], padh), jnp.pad(ck[:, :, 1], padh)], axis=1)
    o_nsa_p = nsa_prompt(q_nsa, gates, ck, kvs_b, kvw_b, batch, seq)

    o_lat_p = mla_prompt(qcat, mla_row_b, batch, seq)

    rows = dec_seq * NSA_HEADS
    q_s = q_nsa[n_p:].reshape(db, rows, HEAD_DIM)
    ch_c = min(64, n_pages)
    o_c, imp = cmp_sample(page_table, q_s, w_cmp, pos_cmp, cache_cmp.reshape(-1, 1, HEAD_DIM), dec_seq, ch_c)
    n_sel = -(-(past + dec_seq) // SEL_BLOCK)
    n_past_blk = past // SEL_BLOCK
    idx_pad, val_pad = select_sample(imp.reshape(n_s, 2 * n_pages), past, dec_seq, n_sel)
    idx = idx_pad[:, :N_SELECT]
    val = val_pad[:, :N_SELECT] > 0
    qpos_s = past + jnp.arange(n_s, dtype=I32) % dec_seq
    kpos = idx[:, :, None] * SEL_BLOCK + jnp.arange(SEL_BLOCK, dtype=I32)
    okmask = (val[:, :, None] & (kpos <= qpos_s[:, None, None])).astype(F32).reshape(db, dec_seq, N_SELECT * SEL_BLOCK)
    kvs_new = kv_s[2 * n_p:].reshape(db, 2 * dec_seq, HEAD_DIM)
    new_blocks = jnp.pad(kvs_new, ((0, 0), (0, ROWS_PER_SEL_BLOCK - 2 * dec_seq), (0, 0)))
    kvw_new = kv_w[2 * n_p:].reshape(db, 2 * dec_seq, HEAD_DIM)
    wnew = jnp.pad(kvw_new, ((0, 0), (0, -(2 * dec_seq) % 16), (0, 0)))
    wk = WINDOW + LANES
    kp = np.concatenate([past - WINDOW + np.arange(WINDOW), past + np.arange(dec_seq),
                         np.full(wk - WINDOW - dec_seq, -1)])
    qp = past + np.arange(rows) // NSA_HEADS
    wmask = ((kp[None, :] <= qp[:, None]) & (kp[None, :] > qp[:, None] - WINDOW) & (kp[None, :] >= 0))
    wmask = jnp.asarray(wmask.astype(np.float32))
    g_s = gates[n_p:, :NSA_GATE_COLS].reshape(db, dec_seq, 3, NSA_HEADS)
    g_rows = jnp.transpose(g_s, (0, 1, 3, 2)).reshape(db, rows, 3)
    win = win_state.reshape(db, 2 * WINDOW, HEAD_DIM)
    o_nsa_s = sel_sample(page_table, idx.reshape(db, dec_seq * N_SELECT), q_s, okmask, o_c, g_rows, win, wnew, wmask,
                         cache_sel.reshape(-1, HEAD_DIM), new_blocks, dec_seq)
    new_win = jnp.concatenate([win[:, 2 * dec_seq:], kvw_new], axis=1)

    qcat_s = qcat[n_p:].reshape(db, dec_seq * MLA_HEADS, MLA_ROW)
    mla_new = jnp.pad(mla_row_b[n_p:].reshape(db, dec_seq, MLA_ROW), ((0, 0), (0, 16 - dec_seq), (0, 0)))
    o_lat_s = mla_sample(page_table, qcat_s, mla_new, cache_mla, dec_seq, min(64, n_pages))

    o_nsa = jnp.concatenate([o_nsa_p, o_nsa_s.reshape(n_s, GROUP_WIDTH)], axis=0)
    o_lat = jnp.concatenate([o_lat_p, o_lat_s.reshape(n_s, MLA_HEADS * MLA_ROW)], axis=0)
    wuv = jnp.pad(jnp.transpose(mla_w_uv, (1, 0, 2)), ((0, 0), (0, MLA_ROPE), (0, 0))).astype(BF16)
    o_mla = head_proj(o_lat, wuv, tm)
    h1 = out_proj(o_nsa, o_mla, group_norm[0:1], group_norm[1:2], w_out.astype(BF16), h, tm, 512)

    w_r = jnp.concatenate([rg_w, re_w, jnp.zeros((d_model, LANES - N_GROUPS - N_EXPERTS), F32)], axis=1)
    b_r = jnp.concatenate([rg_b, re_b, jnp.zeros((LANES - N_GROUPS - N_EXPERTS,), F32)]).reshape(1, LANES)
    xn, ids, gts = router(h1, norm_ffn, w_r, b_r, tm2)
    tmf = 352
    flat_e = ids[:, :TOPK_IN_GROUP].reshape(-1)
    nk = flat_e.shape[0]
    onehot = (flat_e[:, None] == jnp.arange(N_EXPERTS, dtype=I32)[None, :]).astype(I32)
    within = jnp.sum((jnp.cumsum(onehot, axis=0) - onehot) * onehot, axis=1)
    counts = jnp.sum(onehot, axis=0)
    starts = jnp.cumsum(counts) - counts
    order = jnp.zeros((nk,), I32).at[starts[flat_e] + within].set(jnp.arange(nk, dtype=I32))
    blocks_per_expert = (counts + tmf - 1) // tmf
    blk_end = jnp.cumsum(blocks_per_expert)
    n_used = blk_end[-1].astype(I32).reshape(1)
    n_blocks = -(-(nk + N_EXPERTS * (tmf - 1)) // tmf)
    bi = jnp.arange(n_blocks, dtype=I32)
    block_expert = jnp.minimum(jnp.searchsorted(blk_end, bi, side='right'), N_EXPERTS - 1).astype(I32)
    kb = bi - (blk_end - blocks_per_expert)[block_expert]
    row_start = (starts[block_expert] + kb * tmf).astype(I32)
    row_count = jnp.where(bi < n_used[0], jnp.clip(counts[block_expert] - kb * tmf, 0, tmf), 0).astype(I32)
    row_start = jnp.where(bi < n_used[0], row_start, 0)
    y_rows = expert_ffn(block_expert, row_start, row_count, n_used, order, xn, w_gate, w_up, w_down, tmf, 256)
    h2 = moe_combine(h1, y_rows, gts, tm2)

    h3 = ple_add(h2, ple_norm, ple_gate_w.astype(BF16), p_all.astype(BF16), ple_proj.astype(BF16), tm, 512)

    caches = dict(
        cmp_p=kv_c[:2 * n_p].reshape(batch, seq, 2, HEAD_DIM), cmp_s=kv_c[2 * n_p:].reshape(db, dec_seq, 2, HEAD_DIM),
        sel_p=kv_s[:2 * n_p].reshape(batch, seq, 2, HEAD_DIM), sel_s=kv_s[2 * n_p:].reshape(db, dec_seq, 2, HEAD_DIM),
        mla_p=mla_row[:n_p].reshape(batch, seq, MLA_ROW), mla_s=mla_row[n_p:].reshape(db, dec_seq, MLA_ROW),
        win_p=kv_w[:2 * n_p].reshape(batch, 2 * seq, HEAD_DIM)[:, 2 * (seq - min(WINDOW, seq)):].reshape(
            batch, min(WINDOW, seq), 2, HEAD_DIM),
        win_s=new_win.reshape(db, WINDOW, 2, HEAD_DIM))
    return h3[:n_p].reshape(batch, seq, d_model), h3[n_p:].reshape(db, dec_seq, d_model), caches


def kernel(x_prompt, x_sample, p_prompt, p_sample, cache_nsa_cmp, cache_nsa_sel, cache_mla, state_nsa_win, page_table, norm_attn, w_in, nsa_cmp_pos, nsa_cmp_w, mla_q_norm, mla_kv_norm, mla_w_uq, mla_w_uk, mla_w_uv, group_norm, w_out, norm_ffn, router_group_w, router_group_b, router_expert_w, router_expert_b, w_gate, w_up, w_down, ple_proj, ple_norm, ple_gate_w, norm_final):
    depth = w_in.shape[0]
    hp, hs = x_prompt, x_sample
    per_layer = []
    for i in range(depth):
        ple_dim = p_prompt.shape[-1]
        p_all = jnp.concatenate([p_prompt[i].reshape(-1, ple_dim), p_sample[i].reshape(-1, ple_dim)], axis=0)
        wts = (norm_attn[i], w_in[i], nsa_cmp_pos[i], nsa_cmp_w[i], mla_q_norm[i], mla_kv_norm[i], mla_w_uq[i],
               mla_w_uk[i], mla_w_uv[i], group_norm[i], w_out[i], norm_ffn[i], router_group_w[i], router_group_b[i],
               router_expert_w[i], router_expert_b[i], w_gate[i], w_up[i], w_down[i], ple_proj[i], ple_norm[i],
               ple_gate_w[i])
        hp, hs, caches = _layer(hp, hs, p_all, cache_nsa_cmp[i], cache_nsa_sel[i], cache_mla[i], state_nsa_win[i],
                                page_table, wts)
        per_layer.append(caches)
    d_model = hp.shape[-1]
    n_p = hp.shape[0] * hp.shape[1]
    n_s = hs.shape[0] * hs.shape[1]
    y = final_norm(jnp.concatenate([hp.reshape(n_p, d_model), hs.reshape(n_s, d_model)], axis=0), norm_final,
                   int(np.gcd(np.gcd(512, n_s), n_p)))
    y_prompt = y[:n_p].reshape(hp.shape)
    y_sample = y[n_p:].reshape(hs.shape)
    stack = lambda k: jnp.stack([c[k] for c in per_layer])
    return (y_prompt, y_sample, stack('cmp_p'), stack('cmp_s'), stack('sel_p'), stack('sel_s'),
            stack('mla_p'), stack('mla_s'), stack('win_p'), stack('win_s'))
```

```python
import functools

import numpy as np
import jax
import jax.numpy as jnp
from jax import lax
from jax.experimental import pallas as pl
from jax.experimental.pallas import tpu as pltpu

F32 = jnp.float32
BF16 = jnp.bfloat16
I32 = jnp.int32

PAGE_SIZE = 128
HEAD_DIM = 128
NSA_HEADS = 16
CMP_BLOCK = 32
SEL_BLOCK = 64
N_SELECT = 16
WINDOW = 512
MLA_HEADS = 16
MLA_Q_LORA = 896
MLA_KV_LORA = 320
MLA_NOPE = 128
MLA_ROPE = 64
MLA_V = 128
MLA_ROW = MLA_KV_LORA + MLA_ROPE
N_GROUPS = 8
EXPERTS_PER_GROUP = 8
N_EXPERTS = N_GROUPS * EXPERTS_PER_GROUP
TOPK_IN_GROUP = 2
ROPE_THETA = 10000.0
RMS_EPS = 1e-6
QUERY_BLOCK = 128
FORCE_BONUS = 1e4
NEG_INF = -1e30
TINY = 1e-30
NSA_SCALE = HEAD_DIM ** -0.5
MLA_SCALE = (MLA_NOPE + MLA_ROPE) ** -0.5
NSA_Q_COLS = NSA_HEADS * HEAD_DIM
NSA_KV_COLS = 3 * 2 * HEAD_DIM
NSA_GATE_COLS = 3 * NSA_HEADS
GROUP_WIDTH = NSA_HEADS * HEAD_DIM

LANES = 128
VMEM_LIMIT = 56 * 1024 * 1024

ZC_Q = 0
ZC_KV = ZC_Q + NSA_Q_COLS
ZC_CQ = ZC_KV + NSA_KV_COLS
ZC_MLA = ZC_CQ + MLA_Q_LORA
ZC_GATE = ZC_MLA + MLA_ROW
Z_COLS = ZC_GATE + LANES

_CONTRACT_LAST = (((1,), (1,)), ((), ()))


def _cparams(sem, vmem=VMEM_LIMIT):
    return pltpu.CompilerParams(dimension_semantics=sem, vmem_limit_bytes=vmem)


def _rms(x, g):
    return x * lax.rsqrt(jnp.mean(x * x, axis=-1, keepdims=True) + RMS_EPS) * g


def _softmax_heads(s3, bias, maskf):
    s = s3 + bias[None]
    m = jnp.max(s, axis=-1, keepdims=True)
    e = jnp.exp(s - m) * maskf[None]
    return e / jnp.maximum(jnp.sum(e, axis=-1, keepdims=True), TINY)


def _norm_matmul_kernel(x_ref, g_ref, w_ref, o_ref, xn_ref):
    @pl.when(pl.program_id(1) == 0)
    def _():
        xn_ref[...] = _rms(x_ref[...], g_ref[...]).astype(BF16)

    o_ref[...] = jnp.dot(xn_ref[...], w_ref[...], preferred_element_type=F32)


def norm_matmul(x, g, w, tm, tn):
    n, d = x.shape
    c = w.shape[1]
    return pl.pallas_call(
        _norm_matmul_kernel,
        out_shape=jax.ShapeDtypeStruct((n, c), F32),
        grid=(n // tm, c // tn),
        in_specs=[pl.BlockSpec((tm, d), lambda i, j: (i, 0)),
                  pl.BlockSpec((1, d), lambda i, j: (0, 0)),
                  pl.BlockSpec((d, tn), lambda i, j: (0, j))],
        out_specs=pl.BlockSpec((tm, tn), lambda i, j: (i, j)),
        scratch_shapes=[pltpu.VMEM((tm, d), BF16)],
        compiler_params=_cparams(("parallel", "arbitrary")),
        name="norm_in_proj",
    )(x, g.reshape(1, d), w)


def _rope_half(x, cos, sin):
    return x * cos + pltpu.roll(x, 64, 1) * sin


def _rope_quarter(x, cos, sin):
    lane = lax.broadcasted_iota(I32, x.shape, 1)
    partner = jnp.where((lane % 64) < 32, pltpu.roll(x, 96, 1), pltpu.roll(x, 32, 1))
    return x * cos + partner * sin


def _split_kernel(z_ref, c128_ref, s128_ref, c64_ref, s64_ref, gq_ref, gkv_ref, wuq_ref, wuk_ref,
                  q_ref, kvc_ref, kvs_ref, kvw_ref, kvsb_ref, kvwb_ref, gate_ref, qcat_ref, row_ref, rowb_ref):
    c128, s128 = c128_ref[...], s128_ref[...]
    c64, s64 = c64_ref[...], s64_ref[...]
    for h in range(NSA_HEADS):
        x = z_ref[:, ZC_Q + h * 128:ZC_Q + (h + 1) * 128]
        q_ref[:, h * 128:(h + 1) * 128] = (_rope_half(x, c128, s128) * NSA_SCALE).astype(BF16)
    for br, (o32, o16) in enumerate(((kvc_ref, None), (kvs_ref, kvsb_ref), (kvw_ref, kvwb_ref))):
        c0 = ZC_KV + br * 256
        k = _rope_half(z_ref[:, c0:c0 + 128], c128, s128)
        v = z_ref[:, c0 + 128:c0 + 256]
        o32[pl.ds(0, k.shape[0], stride=2), :] = k
        o32[pl.ds(1, k.shape[0], stride=2), :] = v
        if o16 is not None:
            o16[:, 0:128] = k.astype(BF16)
            o16[:, 128:256] = v.astype(BF16)
    gate_ref[...] = jax.nn.sigmoid(z_ref[:, ZC_GATE:ZC_GATE + LANES])

    zc = z_ref[:, ZC_MLA:ZC_MLA + MLA_ROW]
    lane384 = lax.broadcasted_iota(I32, zc.shape, 1)
    sq = jnp.where(lane384 < MLA_KV_LORA, zc * zc, 0.0)
    rstd = lax.rsqrt(jnp.sum(sq, axis=-1, keepdims=True) / MLA_KV_LORA + RMS_EPS)
    normed = zc * rstd * gkv_ref[...]
    x3 = zc[:, 256:384]
    lane128 = lax.broadcasted_iota(I32, x3.shape, 1)
    col3 = jnp.where(lane128 < 64, normed[:, 256:384], _rope_quarter(x3, c64, s64))
    row_ref[:, 0:256] = normed[:, 0:256]
    row_ref[:, 256:384] = col3
    rowb_ref[:, 0:256] = normed[:, 0:256].astype(BF16)
    rowb_ref[:, 256:384] = col3.astype(BF16)

    cq = _rms(z_ref[:, ZC_CQ:ZC_CQ + MLA_Q_LORA], gq_ref[...]).astype(BF16)
    qm = jnp.dot(cq, wuq_ref[...], preferred_element_type=F32)
    pe_base = MLA_HEADS * MLA_NOPE
    for hp in range(MLA_HEADS // 2):
        pe = _rope_quarter(qm[:, pe_base + hp * 128:pe_base + (hp + 1) * 128], c64, s64)
        for sub in range(2):
            h = 2 * hp + sub
            nope = qm[:, h * 128:(h + 1) * 128].astype(BF16)
            ql = jnp.dot(nope, wuk_ref[h], preferred_element_type=F32)
            pe_hi = pe if sub == 1 else pltpu.roll(pe, 64, 1)
            tail = ql[:, 256:384] + jnp.where(lane128 >= 64, pe_hi, 0.0)
            qcat_ref[:, h * 384:h * 384 + 256] = (ql[:, 0:256] * MLA_SCALE).astype(BF16)
            qcat_ref[:, h * 384 + 256:(h + 1) * 384] = (tail * MLA_SCALE).astype(BF16)


def split_projection(z, tabs, tab_index, gq, gkv, wuq, wuk, tm):
    n = z.shape[0]
    c128, s128, c64, s64 = tabs
    row = lambda w: pl.BlockSpec((tm, w), lambda i: (i, 0))
    tab = pl.BlockSpec((tm, LANES), lambda i: (tab_index(i), 0))
    full = lambda a: pl.BlockSpec(a.shape, lambda i: (0,) * a.ndim)
    outs = [((n, NSA_Q_COLS), BF16), ((2 * n, 128), F32), ((2 * n, 128), F32), ((2 * n, 128), F32), ((n, 256), BF16),
            ((n, 256), BF16), ((n, LANES), F32), ((n, MLA_HEADS * MLA_ROW), BF16), ((n, MLA_ROW), F32),
            ((n, MLA_ROW), BF16)]
    return pl.pallas_call(
        _split_kernel,
        out_shape=[jax.ShapeDtypeStruct(s, d) for s, d in outs],
        grid=(n // tm,),
        in_specs=[row(Z_COLS), tab, tab, tab, tab, full(gq), full(gkv), full(wuq), full(wuk)],
        out_specs=[pl.BlockSpec((tm * s[0] // n, s[1]), lambda i: (i, 0)) for s, _ in outs],
        compiler_params=_cparams(("parallel",)),
        name="split_projection",
    )(z, c128, s128, c64, s64, gq, gkv, wuq, wuk)


ROWS_PER_CMP_BLOCK = 2 * CMP_BLOCK


def _compress_blocks(load_rows, pos_ref, w_ref, kv):
    pieces = [(load_rows(2 * l + kv) + pos_ref[2 * l + kv:2 * l + kv + 1, :]).astype(BF16) for l in range(CMP_BLOCK)]
    return jnp.dot(jnp.concatenate(pieces, axis=1), w_ref[kv], preferred_element_type=F32)


def _compress_kernel(x_ref, pos_ref, w_ref, o_ref):
    nblk = o_ref.shape[0]
    for kv in range(2):
        o_ref[:, kv * 128:(kv + 1) * 128] = _compress_blocks(
            lambda r: x_ref[pl.ds(r, nblk, stride=ROWS_PER_CMP_BLOCK), :], pos_ref, w_ref, kv)


def compress_rows(x, pos, w, nblk):
    rows = x.shape[0]
    step = nblk * ROWS_PER_CMP_BLOCK
    return pl.pallas_call(
        _compress_kernel,
        out_shape=jax.ShapeDtypeStruct((rows // ROWS_PER_CMP_BLOCK, 256), F32),
        grid=(rows // step,),
        in_specs=[pl.BlockSpec((step, 128), lambda i: (i, 0)),
                  pl.BlockSpec(pos.shape, lambda i: (0, 0)),
                  pl.BlockSpec(w.shape, lambda i: (0, 0, 0))],
        out_specs=pl.BlockSpec((nblk, 256), lambda i: (i, 0)),
        compiler_params=_cparams(("parallel",)),
        name="compress_prompt",
    )(x, pos, w)


def _nsa_prompt_kernel(q_ref, g_ref, ck_ref, ks_ref, kw_ref, o_ref, *, seq, kc):
    H, QB = NSA_HEADS, QUERY_BLOCK
    n_sel = seq // SEL_BLOCK
    half = ck_ref.shape[1] // 2
    qb = pl.program_id(1)
    start = qb * QB
    Q = jnp.concatenate([q_ref[:, h * 128:(h + 1) * 128] for h in range(H)], axis=0)
    qpos = start + lax.broadcasted_iota(I32, (QB, 1), 0)

    ck = ck_ref[0]
    s_c = lax.dot_general(Q, ck[:, :128].astype(BF16), _CONTRACT_LAST, preferred_element_type=F32)
    lane = lax.broadcasted_iota(I32, (1, 2 * half), 1)
    jj = lane % half
    blk = 2 * jj + (lane >= half).astype(I32)
    mask_c = (((blk + 1) * CMP_BLOCK - 1) <= qpos) & (jj < n_sel)
    p_c = _softmax_heads(s_c.reshape(H, QB, 2 * half), jnp.where(mask_c, 0.0, NEG_INF), mask_c.astype(F32))
    o_c = jnp.dot(p_c.reshape(H * QB, 2 * half).astype(BF16), ck[:, 128:].astype(BF16),
                  preferred_element_type=F32)

    imp = jnp.sum(p_c, axis=0)
    imp = imp[:, :half] + imp[:, half:]
    j = lax.broadcasted_iota(I32, (1, half), 1)
    cur = qpos // SEL_BLOCK
    valid = (j * SEL_BLOCK <= qpos) & (j < n_sel)
    forced = (j == 0) | (j == cur) | (j == cur - 1)
    score = jnp.where(valid, imp + jnp.where(forced, FORCE_BONUS, 0.0), -jnp.inf)
    rank = jnp.zeros((QB, half), F32)
    for i in range(n_sel):
        col = score[:, i:i + 1]
        ahead = (col > score) | ((col == score) & (j > i))
        rank = rank + ahead.astype(F32)
    chosen = ((rank < min(N_SELECT, n_sel)) & valid).astype(BF16)
    jcol = lax.broadcasted_iota(I32, (half, 1), 0)

    def sel_body(c, carry):
        m, l, acc = carry
        k0 = pl.multiple_of(c * kc, kc)
        kk = ks_ref[pl.ds(k0, kc), 0:128]
        vv = ks_ref[pl.ds(k0, kc), 128:256]
        s = lax.dot_general(Q, kk, _CONTRACT_LAST, preferred_element_type=F32).reshape(H, QB, kc)
        kidx = k0 + lax.broadcasted_iota(I32, (1, kc), 1)
        expand = ((kidx // SEL_BLOCK) == jcol).astype(BF16)
        picked = jnp.dot(chosen, expand, preferred_element_type=F32) > 0.5
        ok = picked & (kidx <= qpos)
        s = s + jnp.where(ok, 0.0, NEG_INF)[None]
        m_new = jnp.maximum(m, jnp.max(s, axis=-1, keepdims=True))
        alpha = jnp.exp(m - m_new)
        e = jnp.exp(s - m_new)
        l = alpha * l + jnp.sum(e, axis=-1, keepdims=True)
        pv = jnp.dot(e.reshape(H * QB, kc).astype(BF16), vv, preferred_element_type=F32)
        acc = alpha.reshape(H * QB, 1) * acc + pv
        return m_new, l, acc

    n_chunks = (start + QB + kc - 1) // kc
    m0 = jnp.full((H, QB, 1), NEG_INF, F32)
    l0 = jnp.zeros((H, QB, 1), F32)
    a0 = jnp.zeros((H * QB, 128), F32)
    _, l_s, acc_s = lax.fori_loop(0, n_chunks, sel_body, (m0, l0, a0))
    o_s = acc_s / jnp.maximum(l_s, TINY).reshape(H * QB, 1)

    wk = WINDOW + QB
    w0 = pl.multiple_of(jnp.clip(start - WINDOW, 0, seq - wk), QB)
    kk = kw_ref[pl.ds(w0, wk), 0:128]
    vv = kw_ref[pl.ds(w0, wk), 128:256]
    s_w = lax.dot_general(Q, kk, _CONTRACT_LAST, preferred_element_type=F32).reshape(H, QB, wk)
    kidx = w0 + lax.broadcasted_iota(I32, (1, wk), 1)
    mask_w = (kidx <= qpos) & (kidx > qpos - WINDOW)
    p_w = _softmax_heads(s_w, jnp.where(mask_w, 0.0, NEG_INF), mask_w.astype(F32))
    o_w = jnp.dot(p_w.reshape(H * QB, wk).astype(BF16), vv, preferred_element_type=F32)

    g = g_ref[...]
    for h in range(H):
        r = slice(h * QB, (h + 1) * QB)
        o_ref[:, h * 128:(h + 1) * 128] = (g[:, h:h + 1] * o_c[r] + g[:, H + h:H + h + 1] * o_s[r]
                                           + g[:, 2 * H + h:2 * H + h + 1] * o_w[r])


def nsa_prompt(q, gates, ck, kvs_b, kvw_b, batch, seq, kc=512):
    nqb = seq // QUERY_BLOCK
    assert seq % kc == 0 and seq >= WINDOW + QUERY_BLOCK
    return pl.pallas_call(
        functools.partial(_nsa_prompt_kernel, seq=seq, kc=kc),
        out_shape=jax.ShapeDtypeStruct((batch * seq, GROUP_WIDTH), F32),
        grid=(batch, nqb),
        in_specs=[pl.BlockSpec((QUERY_BLOCK, GROUP_WIDTH), lambda b, i: (b * nqb + i, 0)),
                  pl.BlockSpec((QUERY_BLOCK, LANES), lambda b, i: (b * nqb + i, 0)),
                  pl.BlockSpec((1,) + ck.shape[1:], lambda b, i: (b, 0, 0)),
                  pl.BlockSpec((seq, 256), lambda b, i: (b, 0)),
                  pl.BlockSpec((seq, 256), lambda b, i: (b, 0))],
        out_specs=pl.BlockSpec((QUERY_BLOCK, GROUP_WIDTH), lambda b, i: (b * nqb + i, 0)),
        compiler_params=_cparams(("parallel", "parallel")),
        name="nsa_prompt",
    )(q, gates, ck, kvs_b, kvw_b)


def _mla_prompt_kernel(q_ref, rows_ref, o_ref, *, kc):
    H, QB = MLA_HEADS, QUERY_BLOCK
    start = pl.program_id(1) * QB
    Q = jnp.concatenate([q_ref[:, h * MLA_ROW:(h + 1) * MLA_ROW] for h in range(H)], axis=0)
    qpos = start + lax.broadcasted_iota(I32, (QB, 1), 0)

    def body(c, carry, causal):
        m, l, acc = carry
        k0 = pl.multiple_of(c * kc, kc)
        rows = rows_ref[pl.ds(k0, kc), :]
        s = lax.dot_general(Q, rows, _CONTRACT_LAST, preferred_element_type=F32).reshape(H, QB, kc)
        if causal:
            ok = (k0 + lax.broadcasted_iota(I32, (1, kc), 1)) <= qpos
            s = s + jnp.where(ok, 0.0, NEG_INF)[None]
        m_new = jnp.maximum(m, jnp.max(s, axis=-1, keepdims=True))
        alpha = jnp.exp(m - m_new)
        e = jnp.exp(s - m_new)
        l = alpha * l + jnp.sum(e, axis=-1, keepdims=True)
        pv = jnp.dot(e.reshape(H * QB, kc).astype(BF16), rows, preferred_element_type=F32)
        acc = alpha.reshape(H * QB, 1) * acc + pv
        return m_new, l, acc

    n_full = (start + 1) // kc
    n_chunks = (start + QB + kc - 1) // kc
    m0 = jnp.full((H, QB, 1), NEG_INF, F32)
    l0 = jnp.zeros((H, QB, 1), F32)
    a0 = jnp.zeros((H * QB, MLA_ROW), F32)
    carry = lax.fori_loop(0, n_full, functools.partial(body, causal=False), (m0, l0, a0))
    _, l, acc = lax.fori_loop(n_full, n_chunks, functools.partial(body, causal=True), carry)
    o = (acc / jnp.maximum(l, TINY).reshape(H * QB, 1)).astype(BF16)
    for h in range(H):
        o_ref[:, h * MLA_ROW:(h + 1) * MLA_ROW] = o[h * QB:(h + 1) * QB]


def mla_prompt(qcat, rows_b, batch, seq, kc=512):
    nqb = seq // QUERY_BLOCK
    width = MLA_HEADS * MLA_ROW
    return pl.pallas_call(
        functools.partial(_mla_prompt_kernel, kc=kc),
        out_shape=jax.ShapeDtypeStruct((batch * seq, width), BF16),
        grid=(batch, nqb),
        in_specs=[pl.BlockSpec((QUERY_BLOCK, width), lambda b, i: (b * nqb + i, 0)),
                  pl.BlockSpec((seq, MLA_ROW), lambda b, i: (b, 0))],
        out_specs=pl.BlockSpec((QUERY_BLOCK, width), lambda b, i: (b * nqb + i, 0)),
        compiler_params=_cparams(("parallel", "parallel")),
        name="mla_prompt",
    )(qcat, rows_b)


def _gather_pipeline(pt_ref, page_copies, ch):
    b, c = pl.program_id(0), pl.program_id(1)
    nc = pl.num_programs(1)
    t = b * nc + c
    total = pl.num_programs(0) * nc
    slot = t % 2

    def start(tb, tc, sl):
        for p in range(ch):
            for cp in page_copies(pt_ref[tb, tc * ch + p], p, sl):
                cp.start()

    @pl.when(t == 0)
    def _():
        start(0, 0, 0)

    @pl.when(t + 1 < total)
    def _():
        start((t + 1) // nc, (t + 1) % nc, 1 - slot)

    for p in range(ch):
        for cp in page_copies(0, p, slot):
            cp.wait()
    return slot


def _cmp_sample_kernel(pt_ref, q_ref, w_ref, pos_ref, cache_ref, oc_ref, imp_ref,
                       buf, sem, ck_scr, *, ch, past, dec_seq):
    c = pl.program_id(1)
    blocks_per_page = PAGE_SIZE // CMP_BLOCK
    nsb = ch * blocks_per_page // 2

    def page_copies(page, p, sl):
        copies = []
        for blk in range(blocks_per_page):
            col = (blk % 2) * nsb + p * (blocks_per_page // 2) + blk // 2
            row0 = pl.multiple_of((page * blocks_per_page + blk) * ROWS_PER_CMP_BLOCK, ROWS_PER_CMP_BLOCK)
            copies.append(pltpu.make_async_copy(cache_ref.at[pl.ds(row0, ROWS_PER_CMP_BLOCK)],
                                                buf.at[sl, :, pl.ds(col, 1), :], sem.at[sl]))
        return copies

    slot = _gather_pipeline(pt_ref, page_copies, ch)
    r0 = pl.multiple_of(c * nsb, nsb)
    for kv in range(2):
        ckp = _compress_blocks(lambda r: buf[slot, r], pos_ref, w_ref, kv)
        ck_scr[0, kv, pl.ds(r0, nsb), :] = ckp[:nsb]
        ck_scr[1, kv, pl.ds(r0, nsb), :] = ckp[nsb:]

    @pl.when(c == pl.num_programs(1) - 1)
    def _():
        H = NSA_HEADS
        nsel = ck_scr.shape[2]
        ckk = jnp.concatenate([ck_scr[0, 0], ck_scr[1, 0]], axis=0).astype(BF16)
        ckv = jnp.concatenate([ck_scr[0, 1], ck_scr[1, 1]], axis=0).astype(BF16)
        q = q_ref[0]
        s = lax.dot_general(q, ckk, _CONTRACT_LAST, preferred_element_type=F32)
        lane = lax.broadcasted_iota(I32, (1, 2 * nsel), 1)
        blk = 2 * (lane % nsel) + (lane >= nsel).astype(I32)
        qpos = past + lax.broadcasted_iota(I32, (dec_seq * H, 1), 0) // H
        mask = ((blk + 1) * CMP_BLOCK - 1) <= qpos
        s = jnp.where(mask, s, NEG_INF)
        m = jnp.max(s, axis=-1, keepdims=True)
        e = jnp.where(mask, jnp.exp(s - m), 0.0)
        p = e / jnp.maximum(jnp.sum(e, axis=-1, keepdims=True), TINY)
        oc_ref[0] = jnp.dot(p.astype(BF16), ckv, preferred_element_type=F32)
        imp = jnp.sum(p.reshape(dec_seq, H, 2 * nsel), axis=1)
        imp_ref[0] = imp[:, :nsel] + imp[:, nsel:]


def cmp_sample(page_table, q_s, w_cmp, pos_cmp, cache_rows, dec_seq, ch):
    db, n_pages = page_table.shape
    nsel = 2 * n_pages
    rows = dec_seq * NSA_HEADS
    grid_spec = pltpu.PrefetchScalarGridSpec(
        num_scalar_prefetch=1,
        grid=(db, n_pages // ch),
        in_specs=[pl.BlockSpec((1, rows, 128), lambda b, c, pt: (b, 0, 0)),
                  pl.BlockSpec(w_cmp.shape, lambda b, c, pt: (0, 0, 0)),
                  pl.BlockSpec(pos_cmp.shape, lambda b, c, pt: (0, 0)),
                  pl.BlockSpec(memory_space=pl.ANY)],
        out_specs=[pl.BlockSpec((1, rows, 128), lambda b, c, pt: (b, 0, 0)),
                   pl.BlockSpec((1, dec_seq, nsel), lambda b, c, pt: (b, 0, 0))],
        scratch_shapes=[pltpu.VMEM((2, ROWS_PER_CMP_BLOCK, ch * PAGE_SIZE // CMP_BLOCK, 128), F32),
                        pltpu.SemaphoreType.DMA((2,)),
                        pltpu.VMEM((2, 2, nsel, 128), F32)])
    return pl.pallas_call(
        functools.partial(_cmp_sample_kernel, ch=ch, past=n_pages * PAGE_SIZE, dec_seq=dec_seq),
        out_shape=[jax.ShapeDtypeStruct((db, rows, 128), F32), jax.ShapeDtypeStruct((db, dec_seq, nsel), F32)],
        grid_spec=grid_spec,
        compiler_params=_cparams(("arbitrary", "arbitrary")),
        name="cmp_sample",
    )(page_table, q_s, w_cmp, pos_cmp, cache_rows)


def _select_kernel(imp_ref, idx_ref, val_ref, *, past, dec_seq, n_sel):
    nq, n_past = imp_ref.shape
    x = jnp.concatenate([imp_ref[...], jnp.zeros((nq, LANES), F32)], axis=1)
    width = n_past + LANES
    j = lax.broadcasted_iota(I32, (1, width), 1)
    jf = j.astype(F32)
    qpos = past + lax.broadcasted_iota(I32, (nq, 1), 0) % dec_seq
    cur = qpos // SEL_BLOCK
    valid = (j * SEL_BLOCK <= qpos) & (j < n_sel)
    forced = (j == 0) | (j == cur) | (j == cur - 1)
    score = jnp.where(valid, x + jnp.where(forced, FORCE_BONUS, 0.0), -jnp.inf)
    out_lane = lax.broadcasted_iota(I32, (1, LANES), 1)
    idx = jnp.zeros((nq, LANES), F32)
    val = jnp.zeros((nq, LANES), F32)
    for r in range(min(N_SELECT, n_sel)):
        m = jnp.max(score, axis=-1, keepdims=True)
        ix = jnp.min(jnp.where(score == m, jf, float(width)), axis=-1, keepdims=True)
        idx = jnp.where(out_lane == r, ix, idx)
        val = jnp.where(out_lane == r, (m > -jnp.inf).astype(F32), val)
        score = jnp.where(jf == ix, -jnp.inf, score)
    idx_ref[...] = idx.astype(I32)
    val_ref[...] = val.astype(I32)


def select_sample(imp, past, dec_seq, n_sel):
    nq = imp.shape[0]
    return pl.pallas_call(
        functools.partial(_select_kernel, past=past, dec_seq=dec_seq, n_sel=n_sel),
        out_shape=[jax.ShapeDtypeStruct((nq, LANES), I32), jax.ShapeDtypeStruct((nq, LANES), I32)],
        compiler_params=_cparams(None),
        name="select_sample",
    )(imp)


ROWS_PER_SEL_BLOCK = 2 * SEL_BLOCK


def _sel_copy(cache_ref, new_ref, pt_ref, idx_ref, b, e, buf, slot, sem, n_pages):
    n_past_blk = 2 * n_pages
    ix = idx_ref[b, e]
    dst = buf.at[slot, pl.ds(e * ROWS_PER_SEL_BLOCK, ROWS_PER_SEL_BLOCK), :]
    is_new = ix >= n_past_blk
    page = pt_ref[b, jnp.minimum(ix // 2, n_pages - 1)]
    row0 = pl.multiple_of((2 * page + ix % 2) * ROWS_PER_SEL_BLOCK, ROWS_PER_SEL_BLOCK)

    def new_copy():
        return pltpu.make_async_copy(new_ref.at[b], dst, sem.at[slot])

    def past_copy():
        return pltpu.make_async_copy(cache_ref.at[pl.ds(row0, ROWS_PER_SEL_BLOCK), :], dst, sem.at[slot])

    return is_new, new_copy, past_copy


def _masked_softmax(s, ok):
    s = jnp.where(ok, s, NEG_INF)
    m = jnp.max(s, axis=-1, keepdims=True)
    e = jnp.where(ok, jnp.exp(s - m), 0.0)
    return e / jnp.maximum(jnp.sum(e, axis=-1, keepdims=True), TINY)


def _sel_sample_kernel(pt_ref, idx_ref, q_ref, ok_ref, oc_ref, g_ref, win_ref, wnew_ref, wmask_ref,
                       cache_ref, new_ref, o_ref, buf, sem, kw_scr, *, dec_seq, n_pages):
    H = NSA_HEADS
    nsl = dec_seq * N_SELECT
    b = pl.program_id(0)
    nb = pl.num_programs(0)
    slot = b % 2

    def start(tb, sl):
        for e in range(nsl):
            is_new, new_copy, past_copy = _sel_copy(cache_ref, new_ref, pt_ref, idx_ref, tb, e, buf, sl, sem, n_pages)

            @pl.when(is_new)
            def _():
                new_copy().start()

            @pl.when(jnp.logical_not(is_new))
            def _():
                past_copy().start()

    @pl.when(b == 0)
    def _():
        start(0, 0)

    @pl.when(b + 1 < nb)
    def _():
        start(b + 1, 1 - slot)

    for e in range(nsl):
        pltpu.make_async_copy(new_ref.at[0], buf.at[slot, pl.ds(e * ROWS_PER_SEL_BLOCK, ROWS_PER_SEL_BLOCK), :],
                              sem.at[slot]).wait()

    q = q_ref[0]
    nk = N_SELECT * SEL_BLOCK
    keys = buf[slot, pl.ds(0, nsl * SEL_BLOCK, stride=2), :].astype(BF16).reshape(dec_seq, nk, 128)
    vals = buf[slot, pl.ds(1, nsl * SEL_BLOCK, stride=2), :].astype(BF16).reshape(dec_seq, nk, 128)
    q3 = q.reshape(dec_seq, H, 128)
    s = jnp.einsum('shd,skd->shk', q3, keys, preferred_element_type=F32)
    p = _masked_softmax(s, ok_ref[0][:, None, :] > 0.5)
    o_s = jnp.einsum('shk,skd->shd', p.astype(BF16), vals, preferred_element_type=F32).reshape(dec_seq * H, 128)

    wb = win_ref.shape[1] // 2
    nw = wnew_ref.shape[1] // 2
    for kv in range(2):
        kw_scr[kv, 0:wb, :] = win_ref[0, pl.ds(kv, wb, stride=2), :]
        kw_scr[kv, wb:, :] = jnp.zeros((kw_scr.shape[1] - wb, 128), F32)
        kw_scr[kv, wb:wb + nw, :] = wnew_ref[0, pl.ds(kv, nw, stride=2), :]
    s = lax.dot_general(q, kw_scr[0].astype(BF16), _CONTRACT_LAST, preferred_element_type=F32)
    p = _masked_softmax(s, wmask_ref[...] > 0.5)
    o_w = jnp.dot(p.astype(BF16), kw_scr[1].astype(BF16), preferred_element_type=F32)

    g = g_ref[0]
    o_ref[0] = g[:, 0:1] * oc_ref[0] + g[:, 1:2] * o_s + g[:, 2:3] * o_w


def sel_sample(page_table, idx, q_s, okmask, o_c, gates_rows, win, wnew, wmask, cache_sel, new_blocks, dec_seq):
    db, n_pages = page_table.shape
    rows = dec_seq * NSA_HEADS
    nkeys = N_SELECT * SEL_BLOCK
    wk = wmask.shape[1]
    b3 = lambda shape: pl.BlockSpec((1,) + shape, lambda b, pt, ix: (b, 0, 0))
    grid_spec = pltpu.PrefetchScalarGridSpec(
        num_scalar_prefetch=2,
        grid=(db,),
        in_specs=[b3((rows, 128)), b3((dec_seq, nkeys)), b3((rows, 128)), b3((rows, 3)),
                  b3(win.shape[1:]), b3(wnew.shape[1:]),
                  pl.BlockSpec(wmask.shape, lambda b, pt, ix: (0, 0)),
                  pl.BlockSpec(memory_space=pl.ANY), pl.BlockSpec(memory_space=pl.ANY)],
        out_specs=b3((rows, 128)),
        scratch_shapes=[pltpu.VMEM((2, dec_seq * N_SELECT * ROWS_PER_SEL_BLOCK, 128), F32),
                        pltpu.SemaphoreType.DMA((2,)),
                        pltpu.VMEM((2, wk, 128), F32)])
    return pl.pallas_call(
        functools.partial(_sel_sample_kernel, dec_seq=dec_seq, n_pages=n_pages),
        out_shape=jax.ShapeDtypeStruct((db, rows, 128), F32),
        grid_spec=grid_spec,
        compiler_params=_cparams(("arbitrary",)),
        name="sel_win_sample",
    )(page_table, idx, q_s, okmask, o_c, gates_rows, win, wnew, wmask, cache_sel, new_blocks)


def _mla_sample_kernel(pt_ref, q_ref, new_ref, cache_ref, o_ref, buf, sem, m_scr, l_scr, acc_scr, new_scr,
                       *, ch, dec_seq):
    H = MLA_HEADS
    c = pl.program_id(1)
    def page_copies(page, p, sl):
        return [pltpu.make_async_copy(cache_ref.at[page], buf.at[sl, pl.ds(p * PAGE_SIZE, PAGE_SIZE), :], sem.at[sl])]

    slot = _gather_pipeline(pt_ref, page_copies, ch)
    q = q_ref[0]

    @pl.when(c == 0)
    def _():
        m_scr[...] = jnp.full(m_scr.shape, NEG_INF, F32)
        l_scr[...] = jnp.zeros(l_scr.shape, F32)
        acc_scr[...] = jnp.zeros(acc_scr.shape, F32)

    def update(rows, ok):
        s = lax.dot_general(q, rows, _CONTRACT_LAST, preferred_element_type=F32)
        if ok is not None:
            s = jnp.where(ok, s, NEG_INF)
        m_old = m_scr[...]
        m_new = jnp.maximum(m_old, jnp.max(s, axis=-1, keepdims=True))
        alpha = jnp.exp(m_old - m_new)
        e = jnp.exp(s - m_new)
        if ok is not None:
            e = jnp.where(ok, e, 0.0)
        l_scr[...] = alpha * l_scr[...] + jnp.sum(e, axis=-1, keepdims=True)
        acc_scr[...] = alpha * acc_scr[...] + jnp.dot(e.astype(BF16), rows, preferred_element_type=F32)
        m_scr[...] = m_new

    update(buf[slot].astype(BF16), None)

    @pl.when(c == pl.num_programs(1) - 1)
    def _():
        nr = new_ref.shape[1]
        new_scr[...] = jnp.zeros(new_scr.shape, BF16)
        new_scr[0:nr, :] = new_ref[0]
        t = lax.broadcasted_iota(I32, (1, new_scr.shape[0]), 1)
        srow = lax.broadcasted_iota(I32, (dec_seq * H, 1), 0) // H
        update(new_scr[...], (t <= srow) & (t < dec_seq))
        o_ref[0] = (acc_scr[...] / jnp.maximum(l_scr[...], TINY)).astype(BF16)


def mla_sample(page_table, qcat_s, new_rows, cache, dec_seq, ch):
    db, n_pages = page_table.shape
    rows = dec_seq * MLA_HEADS
    grid_spec = pltpu.PrefetchScalarGridSpec(
        num_scalar_prefetch=1,
        grid=(db, n_pages // ch),
        in_specs=[pl.BlockSpec((1, rows, MLA_ROW), lambda b, c, pt: (b, 0, 0)),
                  pl.BlockSpec((1,) + new_rows.shape[1:], lambda b, c, pt: (b, 0, 0)),
                  pl.BlockSpec(memory_space=pl.ANY)],
        out_specs=pl.BlockSpec((1, rows, MLA_ROW), lambda b, c, pt: (b, 0, 0)),
        scratch_shapes=[pltpu.VMEM((2, ch * PAGE_SIZE, MLA_ROW), F32),
                        pltpu.SemaphoreType.DMA((2,)),
                        pltpu.VMEM((rows, 1), F32), pltpu.VMEM((rows, 1), F32),
                        pltpu.VMEM((rows, MLA_ROW), F32), pltpu.VMEM((LANES, MLA_ROW), BF16)])
    return pl.pallas_call(
        functools.partial(_mla_sample_kernel, ch=ch, dec_seq=dec_seq),
        out_shape=jax.ShapeDtypeStruct((db, rows, MLA_ROW), BF16),
        grid_spec=grid_spec,
        compiler_params=_cparams(("arbitrary", "arbitrary")),
        name="mla_sample",
    )(page_table, qcat_s, new_rows, cache)


def _head_proj_kernel(x_ref, w_ref, o_ref):
    for h in range(MLA_HEADS):
        o_ref[:, h * MLA_V:(h + 1) * MLA_V] = jnp.dot(x_ref[:, h * MLA_ROW:(h + 1) * MLA_ROW], w_ref[h],
                                                      preferred_element_type=F32)


def head_proj(x, w, tm):
    n = x.shape[0]
    return pl.pallas_call(
        _head_proj_kernel,
        out_shape=jax.ShapeDtypeStruct((n, GROUP_WIDTH), F32),
        grid=(n // tm,),
        in_specs=[pl.BlockSpec((tm, x.shape[1]), lambda i: (i, 0)),
                  pl.BlockSpec(w.shape, lambda i: (0, 0, 0))],
        out_specs=pl.BlockSpec((tm, GROUP_WIDTH), lambda i: (i, 0)),
        compiler_params=_cparams(("parallel",)),
        name="mla_value_proj",
    )(x, w)


def _out_proj_kernel(a_ref, b_ref, ga_ref, gb_ref, w_ref, h_ref, o_ref, cat_ref):
    @pl.when(pl.program_id(1) == 0)
    def _():
        cat_ref[:, 0:GROUP_WIDTH] = _rms(a_ref[...], ga_ref[...]).astype(BF16)
        cat_ref[:, GROUP_WIDTH:2 * GROUP_WIDTH] = _rms(b_ref[...], gb_ref[...]).astype(BF16)

    o_ref[...] = h_ref[...] + jnp.dot(cat_ref[...], w_ref[...], preferred_element_type=F32)


def out_proj(o_nsa, o_mla, ga, gb, w, h, tm, tn):
    n, d = h.shape
    gw = GROUP_WIDTH
    return pl.pallas_call(
        _out_proj_kernel,
        out_shape=jax.ShapeDtypeStruct((n, d), F32),
        grid=(n // tm, d // tn),
        in_specs=[pl.BlockSpec((tm, gw), lambda i, j: (i, 0)),
                  pl.BlockSpec((tm, gw), lambda i, j: (i, 0)),
                  pl.BlockSpec((1, gw), lambda i, j: (0, 0)),
                  pl.BlockSpec((1, gw), lambda i, j: (0, 0)),
                  pl.BlockSpec((2 * gw, tn), lambda i, j: (0, j)),
                  pl.BlockSpec((tm, tn), lambda i, j: (i, j))],
        out_specs=pl.BlockSpec((tm, tn), lambda i, j: (i, j)),
        scratch_shapes=[pltpu.VMEM((tm, 2 * gw), BF16)],
        compiler_params=_cparams(("parallel", "arbitrary")),
        name="out_proj",
    )(o_nsa, o_mla, ga, gb, w, h)


def _router_kernel(h_ref, g_ref, w_ref, b_ref, xn_ref, ids_ref, gts_ref):
    xn = _rms(h_ref[...], g_ref[...])
    xn_ref[...] = xn
    logits = jnp.dot(xn, w_ref[...], preferred_element_type=F32, precision=lax.Precision.HIGHEST) + b_ref[...]
    lane = lax.broadcasted_iota(I32, logits.shape, 1)
    lanef = lane.astype(F32)
    big = float(LANES)

    def first_argmax(v):
        m = jnp.max(v, axis=-1, keepdims=True)
        return m, jnp.min(jnp.where(v == m, lanef, big), axis=-1, keepdims=True)

    gl = jnp.where(lane < N_GROUPS, logits, -jnp.inf)
    gmax, gidx = first_argmax(gl)
    g_gate = 1.0 / jnp.sum(jnp.exp(gl - gmax), axis=-1, keepdims=True)
    group_of_lane = ((lane - N_GROUPS) // EXPERTS_PER_GROUP).astype(F32)
    in_group = (lane >= N_GROUPS) & (lane < N_GROUPS + N_EXPERTS) & (group_of_lane == gidx)
    el = jnp.where(in_group, logits, -jnp.inf)
    v0, i0 = first_argmax(el)
    v1, i1 = first_argmax(jnp.where(lanef == i0, -jnp.inf, el))
    e0 = jnp.exp(v0 - v0)
    e1 = jnp.exp(v1 - v0)
    den = e0 + e1
    ids = jnp.where(lane == 0, i0 - N_GROUPS, jnp.where(lane == 1, i1 - N_GROUPS, 0.0))
    gts = jnp.where(lane == 0, g_gate * (e0 / den), jnp.where(lane == 1, g_gate * (e1 / den), 0.0))
    ids_ref[...] = ids.astype(I32)
    gts_ref[...] = gts


def router(h, g, w, b, tm):
    n, d = h.shape
    return pl.pallas_call(
        _router_kernel,
        out_shape=[jax.ShapeDtypeStruct((n, d), F32), jax.ShapeDtypeStruct((n, LANES), I32),
                   jax.ShapeDtypeStruct((n, LANES), F32)],
        grid=(n // tm,),
        in_specs=[pl.BlockSpec((tm, d), lambda i: (i, 0)),
                  pl.BlockSpec((1, d), lambda i: (0, 0)),
                  pl.BlockSpec((d, LANES), lambda i: (0, 0)),
                  pl.BlockSpec((1, LANES), lambda i: (0, 0))],
        out_specs=[pl.BlockSpec((tm, d), lambda i: (i, 0)),
                   pl.BlockSpec((tm, LANES), lambda i: (i, 0)),
                   pl.BlockSpec((tm, LANES), lambda i: (i, 0))],
        compiler_params=_cparams(("parallel",)),
        name="ffn_norm_router",
    )(h, g.reshape(1, d), w, b)


ROW_DMA_WAIT_GROUP = 8
FFN_CAST_CHUNK = 512


def _ffn_kernel(be_ref, rs_ref, cnt_ref, nu_ref, order_ref, x_hbm, wg_ref, wu_ref, wd_ref, y_hbm,
                xf_ref, xb_ref, acc_ref, g_ref, u_ref, h_ref, gsem, ssem, *, n_tok, n_slab):
    i, f = pl.program_id(0), pl.program_id(1)
    nf = pl.num_programs(1)
    nu = nu_ref[0]
    grp = ROW_DMA_WAIT_GROUP

    def gather_row(blk, r):
        tok = order_ref[rs_ref[blk] + r] // 2
        return pltpu.make_async_copy(x_hbm.at[pl.ds(tok, 1), :], xf_ref.at[pl.ds(r, 1), :], gsem)

    def scatter_row(blk, r):
        entry = order_ref[rs_ref[blk] + r]
        dst = (entry % 2) * n_tok + entry // 2
        return pltpu.make_async_copy(acc_ref.at[blk % 2, pl.ds(r, 1), :], y_hbm.at[pl.ds(dst, 1), :], ssem)

    def split(blk):
        return (cnt_ref[blk] // (2 * grp)) * grp

    def start_rows(blk, row_copy, lo, hi):
        n = hi - lo

        def groups(g, carry):
            for u in range(grp):
                row_copy(blk, lo + g * grp + u).start()
            return carry

        def singles(r, carry):
            row_copy(blk, lo + (n // grp) * grp + r).start()
            return carry
        lax.fori_loop(0, n // grp, groups, 0)
        lax.fori_loop(0, n % grp, singles, 0)

    def wait_rows(blk, row_copy, group_copy):
        n = cnt_ref[blk]

        def groups(g, carry):
            group_copy().wait()
            return carry

        def singles(r, carry):
            row_copy(0, 0).wait()
            return carry
        lax.fori_loop(0, n // grp, groups, 0)
        lax.fori_loop(0, n % grp, singles, 0)

    def gather_group():
        return pltpu.make_async_copy(x_hbm.at[pl.ds(0, grp), :], xf_ref.at[pl.ds(0, grp), :], gsem)

    def scatter_group():
        return pltpu.make_async_copy(acc_ref.at[0, pl.ds(0, grp), :], y_hbm.at[pl.ds(0, grp), :], ssem)

    used = i < nu

    @pl.when(used & (f == 0))
    def _():
        @pl.when(i == 0)
        def _():
            xf_ref[...] = jnp.zeros(xf_ref.shape, F32)
            acc_ref[...] = jnp.zeros(acc_ref.shape, F32)
            start_rows(0, gather_row, 0, cnt_ref[0])

        wait_rows(i, gather_row, gather_group)
        xb_ref[...] = xf_ref[...].astype(BF16)

        @pl.when(i + 1 < nu)
        def _():
            start_rows(i + 1, gather_row, 0, split(i + 1))

        @pl.when(i > 1)
        def _():
            wait_rows(i - 2, scatter_row, scatter_group)

    @pl.when(used & (f == 1))
    def _():
        @pl.when(i + 1 < nu)
        def _():
            start_rows(i + 1, gather_row, split(i + 1), cnt_ref[i + 1])

        @pl.when(i > 0)
        def _():
            start_rows(i - 1, scatter_row, 0, split(i - 1))

    @pl.when(used & (f == 2) & (i > 0))
    def _():
        start_rows(i - 1, scatter_row, split(i - 1), cnt_ref[i - 1])

    slab = wg_ref.shape[0]
    for c in range(n_slab):
        @pl.when(used & (f == c))
        def _(c=c):
            gate = up = None
            for k0 in range(0, slab, FFN_CAST_CHUNK):
                xk = xb_ref[:, c * slab + k0:c * slab + k0 + FFN_CAST_CHUNK]
                g_part = jnp.dot(xk, wg_ref[k0:k0 + FFN_CAST_CHUNK, :].astype(BF16), preferred_element_type=F32)
                u_part = jnp.dot(xk, wu_ref[k0:k0 + FFN_CAST_CHUNK, :].astype(BF16), preferred_element_type=F32)
                gate = g_part if gate is None else gate + g_part
                up = u_part if up is None else up + u_part
            if c == 0:
                g_ref[...] = gate
                u_ref[...] = up
            else:
                g_ref[...] += gate
                u_ref[...] += up
            if c == n_slab - 1:
                h_ref[...] = (jax.nn.silu(g_ref[...]) * u_ref[...]).astype(BF16)

        @pl.when(used & (f == n_slab + c))
        def _(c=c):
            acc = acc_ref.at[i % 2]
            hmid = h_ref[...]
            for n0 in range(0, slab, FFN_CAST_CHUNK):
                acc[:, c * slab + n0:c * slab + n0 + FFN_CAST_CHUNK] = jnp.dot(
                    hmid, wd_ref[:, n0:n0 + FFN_CAST_CHUNK].astype(BF16), preferred_element_type=F32)

    @pl.when(used)
    def _():
        @pl.when((f == nf - 1) & (i == nu - 1))
        def _():
            start_rows(i, scatter_row, 0, cnt_ref[i])

            @pl.when(i > 0)
            def _():
                wait_rows(i - 1, scatter_row, scatter_group)
            wait_rows(i, scatter_row, scatter_group)


def expert_ffn(block_expert, row_start, row_count, n_used, order, xn, w_gate, w_up, w_down, tm, tf):
    n_tok, d = xn.shape
    d_exp = w_gate.shape[2]
    slab = tf
    assert d % slab == 0
    n_slab = d // slab
    nf = 2 * n_slab
    assert nf >= 4, "the row DMA schedule uses steps 0, 1, 2 and the last step of a block"
    n_blocks = block_expert.shape[0]

    def blk(i, nu):
        return jnp.minimum(i, nu[0] - 1)

    def k_idx(i, f, nu):
        return jnp.where(i < nu[0], jnp.minimum(f, n_slab - 1), n_slab - 1)

    def n_idx(i, f, nu):
        return jnp.where(i < nu[0], jnp.maximum(f - n_slab, 0), n_slab - 1)

    grid_spec = pltpu.PrefetchScalarGridSpec(
        num_scalar_prefetch=5,
        grid=(n_blocks, nf),
        in_specs=[pl.BlockSpec(memory_space=pl.ANY),
                  pl.BlockSpec((None, slab, d_exp), lambda i, f, be, rs, cn, nu, od: (be[blk(i, nu)], k_idx(i, f, nu), 0)),
                  pl.BlockSpec((None, slab, d_exp), lambda i, f, be, rs, cn, nu, od: (be[blk(i, nu)], k_idx(i, f, nu), 0)),
                  pl.BlockSpec((None, d_exp, slab), lambda i, f, be, rs, cn, nu, od: (be[blk(i, nu)], 0, n_idx(i, f, nu)))],
        out_specs=pl.BlockSpec(memory_space=pl.ANY),
        scratch_shapes=[pltpu.VMEM((tm, d), F32), pltpu.VMEM((tm, d), BF16), pltpu.VMEM((2, tm, d), F32),
                        pltpu.VMEM((tm, d_exp), F32), pltpu.VMEM((tm, d_exp), F32), pltpu.VMEM((tm, d_exp), BF16),
                        pltpu.SemaphoreType.DMA(()), pltpu.SemaphoreType.DMA(())])
    return pl.pallas_call(
        functools.partial(_ffn_kernel, n_tok=n_tok, n_slab=n_slab),
        out_shape=jax.ShapeDtypeStruct((TOPK_IN_GROUP * n_tok, d), F32),
        grid_spec=grid_spec,
        compiler_params=_cparams(("arbitrary", "arbitrary")),
        name="expert_ffn",
    )(block_expert, row_start, row_count, n_used, order, xn, w_gate, w_up, w_down)


def _combine_kernel(h_ref, y0_ref, y1_ref, g_ref, o_ref):
    g = g_ref[...]
    o_ref[...] = h_ref[...] + (g[:, 0:1] * y0_ref[...] + g[:, 1:2] * y1_ref[...])


def moe_combine(h, y, gts, tm):
    n, d = h.shape
    nt = n // tm
    return pl.pallas_call(
        _combine_kernel,
        out_shape=jax.ShapeDtypeStruct((n, d), F32),
        grid=(nt,),
        in_specs=[pl.BlockSpec((tm, d), lambda i: (i, 0)),
                  pl.BlockSpec((tm, d), lambda i: (i, 0)),
                  pl.BlockSpec((tm, d), lambda i: (i + nt, 0)),
                  pl.BlockSpec((tm, LANES), lambda i: (i, 0))],
        out_specs=pl.BlockSpec((tm, d), lambda i: (i, 0)),
        compiler_params=_cparams(("parallel",)),
        name="moe_combine",
    )(h, y, y, gts)


def _ple_kernel(h_ref, hc_ref, g_ref, wg_ref, p_ref, wp_ref, o_ref, hn_ref):
    @pl.when(pl.program_id(1) == 0)
    def _():
        hn_ref[...] = _rms(h_ref[...], g_ref[...]).astype(BF16)

    gate = jax.nn.sigmoid(jnp.dot(hn_ref[...], wg_ref[...], preferred_element_type=F32))
    proj = jnp.dot(p_ref[...], wp_ref[...], preferred_element_type=F32)
    o_ref[...] = hc_ref[...] + gate * proj


def ple_add(h, g, wg, p, wp, tm, tn):
    n, d = h.shape
    pd = p.shape[1]
    return pl.pallas_call(
        _ple_kernel,
        out_shape=jax.ShapeDtypeStruct((n, d), F32),
        grid=(n // tm, d // tn),
        in_specs=[pl.BlockSpec((tm, d), lambda i, j: (i, 0)),
                  pl.BlockSpec((tm, tn), lambda i, j: (i, j)),
                  pl.BlockSpec((1, d), lambda i, j: (0, 0)),
                  pl.BlockSpec((d, tn), lambda i, j: (0, j)),
                  pl.BlockSpec((tm, pd), lambda i, j: (i, 0)),
                  pl.BlockSpec((pd, tn), lambda i, j: (0, j))],
        out_specs=pl.BlockSpec((tm, tn), lambda i, j: (i, j)),
        scratch_shapes=[pltpu.VMEM((tm, d), BF16)],
        compiler_params=_cparams(("parallel", "arbitrary")),
        name="ple_add",
    )(h, h, g.reshape(1, d), wg, p, wp)


def _final_norm_kernel(x_ref, g_ref, o_ref):
    o_ref[...] = _rms(x_ref[...], g_ref[...])


def final_norm(x, g, tm):
    n, d = x.shape
    return pl.pallas_call(
        _final_norm_kernel,
        out_shape=jax.ShapeDtypeStruct((n, d), F32),
        grid=(n // tm,),
        in_specs=[pl.BlockSpec((tm, d), lambda i: (i, 0)), pl.BlockSpec((1, d), lambda i: (0, 0))],
        out_specs=pl.BlockSpec((tm, d), lambda i: (i, 0)),
        compiler_params=_cparams(("parallel",)),
        name="final_norm",
    )(x, g.reshape(1, d))


def _rope_tables(pos, tm):
    pos = np.asarray(pos, np.float32)[:, None]

    def table(dim):
        inv = np.float32(ROPE_THETA) ** (-np.arange(0, dim, 2, dtype=np.float32) / np.float32(dim))
        ang = (pos * inv[None, :]).astype(np.float32).astype(np.float64)
        cos, sin = np.cos(ang).astype(np.float32), np.sin(ang).astype(np.float32)
        reps = LANES // dim
        return np.tile(np.concatenate([cos, cos], 1), (1, reps)), np.tile(np.concatenate([-sin, sin], 1), (1, reps))

    c128, s128 = table(HEAD_DIM)
    c64, s64 = table(MLA_ROPE)
    return tuple(jnp.asarray(t) for t in (c128, s128, c64, s64))


def _compress_weights(cmp_pos, cmp_w):
    w = cmp_w.reshape(2, CMP_BLOCK * HEAD_DIM, HEAD_DIM).astype(BF16)
    pos = jnp.transpose(cmp_pos, (1, 0, 2)).reshape(ROWS_PER_CMP_BLOCK, HEAD_DIM)
    return w, pos


def _layer(hp, hs, p_all, cache_cmp, cache_sel, cache_mla, win_state, page_table, wts):
    (norm_attn, w_in, cmp_pos, cmp_w, mla_q_norm, mla_kv_norm, mla_w_uq, mla_w_uk, mla_w_uv, group_norm, w_out,
     norm_ffn, rg_w, rg_b, re_w, re_b, w_gate, w_up, w_down, ple_proj, ple_norm, ple_gate_w) = wts
    batch, seq, d_model = hp.shape
    db, dec_seq, _ = hs.shape
    n_pages = page_table.shape[1]
    past = n_pages * PAGE_SIZE
    n_p, n_s = batch * seq, db * dec_seq
    n = n_p + n_s
    tm = int(np.gcd(np.gcd(512, n_s), seq))
    assert n_p % tm == 0 and n_s % tm == 0 and seq % tm == 0 and tm % dec_seq == 0
    assert dec_seq < CMP_BLOCK and dec_seq <= SEL_BLOCK and past % SEL_BLOCK == 0 and past >= WINDOW
    assert win_state.shape[1] == WINDOW

    h = jnp.concatenate([hp.reshape(n_p, d_model), hs.reshape(n_s, d_model)], axis=0)

    c0, c1 = NSA_Q_COLS, NSA_Q_COLS + NSA_KV_COLS
    c2 = c1 + NSA_GATE_COLS
    w_in_p = jnp.concatenate([w_in[:, :c1], w_in[:, c2:], w_in[:, c1:c2],
                              jnp.zeros((d_model, LANES - NSA_GATE_COLS), w_in.dtype)], axis=1).astype(BF16)
    z = norm_matmul(h, norm_attn, w_in_p, tm, 384)

    tm2 = min(256, tm)
    pos_rows = np.concatenate([np.arange(seq), past + np.arange(tm2) % dec_seq])
    tabs = _rope_tables(pos_rows, tm2)
    n_ptiles, seq_tiles = n_p // tm2, seq // tm2
    tab_index = lambda i: jnp.where(i < n_ptiles, i % seq_tiles, seq_tiles)
    wuq = jnp.concatenate([mla_w_uq[:, :, :MLA_NOPE].reshape(MLA_Q_LORA, -1),
                           mla_w_uq[:, :, MLA_NOPE:].reshape(MLA_Q_LORA, -1)], axis=1).astype(BF16)
    wuk = jnp.pad(jnp.transpose(mla_w_uk, (1, 2, 0)), ((0, 0), (0, 0), (0, MLA_ROPE))).astype(BF16)
    gkv = jnp.pad(mla_kv_norm, (0, MLA_ROPE)).reshape(1, MLA_ROW)
    (q_nsa, kv_c, kv_s, kv_w, kvs_b, kvw_b, gates, qcat, mla_row, mla_row_b) = split_projection(
        z, tabs, tab_index, mla_q_norm.reshape(1, -1), gkv, wuq, wuk, tm2)

    w_cmp, pos_cmp = _compress_weights(cmp_pos, cmp_w)
    nb = seq // CMP_BLOCK
    ck = compress_rows(kv_c[:2 * n_p], pos_cmp, w_cmp, min(256, n_p // CMP_BLOCK))
    half = -(-(nb // 2) // LANES) * LANES
    ck = ck.reshape(batch, nb // 2, 2, 256)
    padh = ((0, 0), (0, half - nb // 2), (0, 0))
    ck = jnp.concatenate([jnp.pad(ck[:, :, 0], padh), jnp.pad(ck[:, :, 1], padh)], axis=1)
    o_nsa_p = nsa_prompt(q_nsa, gates, ck, kvs_b, kvw_b, batch, seq)

    o_lat_p = mla_prompt(qcat, mla_row_b, batch, seq)

    rows = dec_seq * NSA_HEADS
    q_s = q_nsa[n_p:].reshape(db, rows, HEAD_DIM)
    ch_c = min(64, n_pages)
    o_c, imp = cmp_sample(page_table, q_s, w_cmp, pos_cmp, cache_cmp.reshape(-1, 1, HEAD_DIM), dec_seq, ch_c)
    n_sel = -(-(past + dec_seq) // SEL_BLOCK)
    n_past_blk = past // SEL_BLOCK
    idx_pad, val_pad = select_sample(imp.reshape(n_s, 2 * n_pages), past, dec_seq, n_sel)
    idx = idx_pad[:, :N_SELECT]
    val = val_pad[:, :N_SELECT] > 0
    qpos_s = past + jnp.arange(n_s, dtype=I32) % dec_seq
    kpos = idx[:, :, None] * SEL_BLOCK + jnp.arange(SEL_BLOCK, dtype=I32)
    okmask = (val[:, :, None] & (kpos <= qpos_s[:, None, None])).astype(F32).reshape(db, dec_seq, N_SELECT * SEL_BLOCK)
    kvs_new = kv_s[2 * n_p:].reshape(db, 2 * dec_seq, HEAD_DIM)
    new_blocks = jnp.pad(kvs_new, ((0, 0), (0, ROWS_PER_SEL_BLOCK - 2 * dec_seq), (0, 0)))
    kvw_new = kv_w[2 * n_p:].reshape(db, 2 * dec_seq, HEAD_DIM)
    wnew = jnp.pad(kvw_new, ((0, 0), (0, -(2 * dec_seq) % 16), (0, 0)))
    wk = WINDOW + LANES
    kp = np.concatenate([past - WINDOW + np.arange(WINDOW), past + np.arange(dec_seq),
                         np.full(wk - WINDOW - dec_seq, -1)])
    qp = past + np.arange(rows) // NSA_HEADS
    wmask = ((kp[None, :] <= qp[:, None]) & (kp[None, :] > qp[:, None] - WINDOW) & (kp[None, :] >= 0))
    wmask = jnp.asarray(wmask.astype(np.float32))
    g_s = gates[n_p:, :NSA_GATE_COLS].reshape(db, dec_seq, 3, NSA_HEADS)
    g_rows = jnp.transpose(g_s, (0, 1, 3, 2)).reshape(db, rows, 3)
    win = win_state.reshape(db, 2 * WINDOW, HEAD_DIM)
    o_nsa_s = sel_sample(page_table, idx.reshape(db, dec_seq * N_SELECT), q_s, okmask, o_c, g_rows, win, wnew, wmask,
                         cache_sel.reshape(-1, HEAD_DIM), new_blocks, dec_seq)
    new_win = jnp.concatenate([win[:, 2 * dec_seq:], kvw_new], axis=1)

    qcat_s = qcat[n_p:].reshape(db, dec_seq * MLA_HEADS, MLA_ROW)
    mla_new = jnp.pad(mla_row_b[n_p:].reshape(db, dec_seq, MLA_ROW), ((0, 0), (0, 16 - dec_seq), (0, 0)))
    o_lat_s = mla_sample(page_table, qcat_s, mla_new, cache_mla, dec_seq, min(64, n_pages))

    o_nsa = jnp.concatenate([o_nsa_p, o_nsa_s.reshape(n_s, GROUP_WIDTH)], axis=0)
    o_lat = jnp.concatenate([o_lat_p, o_lat_s.reshape(n_s, MLA_HEADS * MLA_ROW)], axis=0)
    wuv = jnp.pad(jnp.transpose(mla_w_uv, (1, 0, 2)), ((0, 0), (0, MLA_ROPE), (0, 0))).astype(BF16)
    o_mla = head_proj(o_lat, wuv, tm)
    h1 = out_proj(o_nsa, o_mla, group_norm[0:1], group_norm[1:2], w_out.astype(BF16), h, tm, 512)

    w_r = jnp.concatenate([rg_w, re_w, jnp.zeros((d_model, LANES - N_GROUPS - N_EXPERTS), F32)], axis=1)
    b_r = jnp.concatenate([rg_b, re_b, jnp.zeros((LANES - N_GROUPS - N_EXPERTS,), F32)]).reshape(1, LANES)
    xn, ids, gts = router(h1, norm_ffn, w_r, b_r, tm2)
    tmf = 352
    flat_e = ids[:, :TOPK_IN_GROUP].reshape(-1)
    nk = flat_e.shape[0]
    onehot = (flat_e[:, None] == jnp.arange(N_EXPERTS, dtype=I32)[None, :]).astype(I32)
    within = jnp.sum((jnp.cumsum(onehot, axis=0) - onehot) * onehot, axis=1)
    counts = jnp.sum(onehot, axis=0)
    starts = jnp.cumsum(counts) - counts
    order = jnp.zeros((nk,), I32).at[starts[flat_e] + within].set(jnp.arange(nk, dtype=I32))
    blocks_per_expert = (counts + tmf - 1) // tmf
    blk_end = jnp.cumsum(blocks_per_expert)
    n_used = blk_end[-1].astype(I32).reshape(1)
    n_blocks = -(-(nk + N_EXPERTS * (tmf - 1)) // tmf)
    bi = jnp.arange(n_blocks, dtype=I32)
    block_expert = jnp.minimum(jnp.searchsorted(blk_end, bi, side='right'), N_EXPERTS - 1).astype(I32)
    kb = bi - (blk_end - blocks_per_expert)[block_expert]
    row_start = (starts[block_expert] + kb * tmf).astype(I32)
    row_count = jnp.where(bi < n_used[0], jnp.clip(counts[block_expert] - kb * tmf, 0, tmf), 0).astype(I32)
    row_start = jnp.where(bi < n_used[0], row_start, 0)
    y_rows = expert_ffn(block_expert, row_start, row_count, n_used, order, xn, w_gate, w_up, w_down, tmf, 1024)
    h2 = moe_combine(h1, y_rows, gts, tm2)

    h3 = ple_add(h2, ple_norm, ple_gate_w.astype(BF16), p_all.astype(BF16), ple_proj.astype(BF16), tm, 512)

    caches = dict(
        cmp_p=kv_c[:2 * n_p].reshape(batch, seq, 2, HEAD_DIM), cmp_s=kv_c[2 * n_p:].reshape(db, dec_seq, 2, HEAD_DIM),
        sel_p=kv_s[:2 * n_p].reshape(batch, seq, 2, HEAD_DIM), sel_s=kv_s[2 * n_p:].reshape(db, dec_seq, 2, HEAD_DIM),
        mla_p=mla_row[:n_p].reshape(batch, seq, MLA_ROW), mla_s=mla_row[n_p:].reshape(db, dec_seq, MLA_ROW),
        win_p=kv_w[:2 * n_p].reshape(batch, 2 * seq, HEAD_DIM)[:, 2 * (seq - min(WINDOW, seq)):].reshape(
            batch, min(WINDOW, seq), 2, HEAD_DIM),
        win_s=new_win.reshape(db, WINDOW, 2, HEAD_DIM))
    return h3[:n_p].reshape(batch, seq, d_model), h3[n_p:].reshape(db, dec_seq, d_model), caches


def kernel(x_prompt, x_sample, p_prompt, p_sample, cache_nsa_cmp, cache_nsa_sel, cache_mla, state_nsa_win, page_table, norm_attn, w_in, nsa_cmp_pos, nsa_cmp_w, mla_q_norm, mla_kv_norm, mla_w_uq, mla_w_uk, mla_w_uv, group_norm, w_out, norm_ffn, router_group_w, router_group_b, router_expert_w, router_expert_b, w_gate, w_up, w_down, ple_proj, ple_norm, ple_gate_w, norm_final):
    depth = w_in.shape[0]
    hp, hs = x_prompt, x_sample
    per_layer = []
    for i in range(depth):
        ple_dim = p_prompt.shape[-1]
        p_all = jnp.concatenate([p_prompt[i].reshape(-1, ple_dim), p_sample[i].reshape(-1, ple_dim)], axis=0)
        wts = (norm_attn[i], w_in[i], nsa_cmp_pos[i], nsa_cmp_w[i], mla_q_norm[i], mla_kv_norm[i], mla_w_uq[i],
               mla_w_uk[i], mla_w_uv[i], group_norm[i], w_out[i], norm_ffn[i], router_group_w[i], router_group_b[i],
               router_expert_w[i], router_expert_b[i], w_gate[i], w_up[i], w_down[i], ple_proj[i], ple_norm[i],
               ple_gate_w[i])
        hp, hs, caches = _layer(hp, hs, p_all, cache_nsa_cmp[i], cache_nsa_sel[i], cache_mla[i], state_nsa_win[i],
                                page_table, wts)
        per_layer.append(caches)
    d_model = hp.shape[-1]
    n_p = hp.shape[0] * hp.shape[1]
    n_s = hs.shape[0] * hs.shape[1]
    y = final_norm(jnp.concatenate([hp.reshape(n_p, d_model), hs.reshape(n_s, d_model)], axis=0), norm_final,
                   int(np.gcd(np.gcd(512, n_s), n_p)))
    y_prompt = y[:n_p].reshape(hp.shape)
    y_sample = y[n_p:].reshape(hs.shape)
    stack = lambda k: jnp.stack([c[k] for c in per_layer])
    return (y_prompt, y_sample, stack('cmp_p'), stack('cmp_s'), stack('sel_p'), stack('sel_s'),
            stack('mla_p'), stack('mla_s'), stack('win_p'), stack('win_s'))
```
